```python
import math
import jax, jax.numpy as jnp
from jax import lax
import numpy as np

D_MODEL = 2048
BATCH = 1
SEQ = 8192
DEPTH = 4

N_MEM = 256
C_A = 1024
N_GROUPS_A = 8
CONV_A_WIDTH = 31
H_DIFF = 8
DH_DIFF = 64
QK_WIDTH = 2 * H_DIFF * DH_DIFF
V_WIDTH = H_DIFF * 2 * DH_DIFF
MIX_WIDTH = C_A + V_WIDTH
D_IN = 2 * C_A + 2 * QK_WIDTH + V_WIDTH
N_BUCKETS = 32
MAX_DISTANCE = 128
Q_BLOCK = 128
H_CROSS = 4
DH_CROSS = 128
D_FF = 5632
CONV_F_WIDTH = 3
NORM_EPS = 1e-6
SUBLN_EPS = 1e-5
NEG_INF = -1e30

kernel_name = "hybrid_conformer_diffattn_trunk"


def rms_norm(x, g, eps=NORM_EPS):
    xf = x.astype(jnp.float32)
    y = xf * lax.rsqrt(jnp.mean(xf * xf, axis=-1, keepdims=True) + eps)
    return (y * g.astype(jnp.float32)).astype(x.dtype)


def layer_norm(x, g, b, eps=NORM_EPS):
    xf = x.astype(jnp.float32)
    mu = jnp.mean(xf, axis=-1, keepdims=True)
    xc = xf - mu
    y = xc * lax.rsqrt(jnp.mean(xc * xc, axis=-1, keepdims=True) + eps)
    return (y * g.astype(jnp.float32) + b.astype(jnp.float32)).astype(x.dtype)


def causal_depthwise_conv(u, w, b):
    k_w, c = w.shape
    y = lax.conv_general_dilated(
        u, w[:, None, :].astype(u.dtype), window_strides=(1,), padding=[(k_w - 1, 0)],
        dimension_numbers=("NWC", "WIO", "NWC"), feature_group_count=c)
    return y + b.astype(u.dtype)


def t5_causal_bucket(n):
    max_exact = N_BUCKETS // 2
    nf = jnp.maximum(n, 1).astype(jnp.float32)
    large = max_exact + (jnp.log(nf / max_exact) / math.log(MAX_DISTANCE / max_exact)
                         * (N_BUCKETS - max_exact)).astype(jnp.int32)
    large = jnp.minimum(large, N_BUCKETS - 1)
    return jnp.where(n < max_exact, n, large)


def diff_attention(q, k, v, lam, bias_dist_t):
    b, s = q.shape[0], q.shape[1]
    n_blocks = s // Q_BLOCK
    scale = DH_DIFF ** -0.5
    q_blocks = q.reshape(b, n_blocks, Q_BLOCK, 2 * H_DIFF, DH_DIFF).transpose(1, 0, 2, 3, 4)
    k_pos = jnp.arange(s)

    def one_block(args):
        q_blk, i = args
        q_pos = i * Q_BLOCK + jnp.arange(Q_BLOCK)
        rel = q_pos[:, None] - k_pos[None, :]
        causal = rel >= 0
        bias = bias_dist_t[:, jnp.clip(rel, 0, s - 1)].astype(jnp.float32)
        logits = jnp.einsum("bqnd,bknd->bnqk", q_blk, k).astype(jnp.float32) * scale
        logits = logits.reshape(b, H_DIFF, 2, Q_BLOCK, s) + bias[None, :, None]
        logits = jnp.where(causal[None, None, None], logits, NEG_INF)
        p = jax.nn.softmax(logits, axis=-1)
        w = p[:, :, 0] - lam * p[:, :, 1]
        return jnp.einsum("bhqk,bkhe->bqhe", w.astype(v.dtype), v)

    out = lax.map(one_block, (q_blocks, jnp.arange(n_blocks)))
    return out.transpose(1, 0, 2, 3, 4).reshape(b, s, H_DIFF, 2 * DH_DIFF)


def setup_inputs(seed: int = 0) -> dict:
    key = jax.random.key(seed)
    ks = iter(jax.random.split(key, 32))
    f32 = jnp.float32

    def nrm(shape, scale):
        return jax.random.normal(next(ks), shape, f32) * scale

    def gain(shape):
        return 1.0 + 0.01 * jax.random.normal(next(ks), shape, f32)

    return {
        "x": nrm((BATCH, SEQ, D_MODEL), 1.0),
        "mem": nrm((BATCH, N_MEM, D_MODEL), 1.0),
        "rel_bias_table": nrm((N_BUCKETS, H_DIFF), 0.5),
        "g_mix": gain((DEPTH, D_MODEL)),
        "w_in": nrm((DEPTH, D_MODEL, D_IN), D_MODEL ** -0.5),
        "conv_a_w": nrm((DEPTH, CONV_A_WIDTH, C_A), CONV_A_WIDTH ** -0.5),
        "conv_a_b": nrm((DEPTH, C_A), 0.01),
        "ln_a_g": gain((DEPTH, C_A)),
        "ln_a_b": nrm((DEPTH, C_A), 0.01),
        "diff_lambda": nrm((DEPTH, 4, DH_DIFF), 0.1),
        "subln_g": gain((DEPTH, 2 * DH_DIFF)),
        "w_out": nrm((DEPTH, MIX_WIDTH, D_MODEL), MIX_WIDTH ** -0.5),
        "g_cross": gain((DEPTH, D_MODEL)),
        "g_mem": gain((DEPTH, D_MODEL)),
        "w_cq": nrm((DEPTH, D_MODEL, H_CROSS * DH_CROSS), D_MODEL ** -0.5),
        "w_ckv": nrm((DEPTH, D_MODEL, 2 * H_CROSS * DH_CROSS), D_MODEL ** -0.5),
        "w_co": nrm((DEPTH, H_CROSS * DH_CROSS, D_MODEL), (H_CROSS * DH_CROSS) ** -0.5),
        "g_ffn": gain((DEPTH, D_MODEL)),
        "w_up": nrm((DEPTH, D_MODEL, 2 * D_FF), D_MODEL ** -0.5),
        "conv_f_w": nrm((DEPTH, CONV_F_WIDTH, 2 * D_FF), CONV_F_WIDTH ** -0.5),
        "conv_f_b": nrm((DEPTH, 2 * D_FF), 0.01),
        "w_down": nrm((DEPTH, D_FF, D_MODEL), D_FF ** -0.5),
        "g_final": gain((D_MODEL,)),
    }


def reference(x, mem, rel_bias_table, g_mix, w_in, conv_a_w, conv_a_b, ln_a_g, ln_a_b,
              diff_lambda, subln_g, w_out, g_cross, g_mem, w_cq, w_ckv, w_co,
              g_ffn, w_up, conv_f_w, conv_f_b, w_down, g_final):
    b, s, _ = x.shape
    bias_dist_t = rel_bias_table[t5_causal_bucket(jnp.arange(s, dtype=jnp.int32))].T

    for l in range(DEPTH):
        h = rms_norm(x, g_mix[l])
        proj = h @ w_in[l]
        a_in = proj[..., :2 * C_A]
        o = 2 * C_A
        q = proj[..., o:o + QK_WIDTH].reshape(b, s, 2 * H_DIFF, DH_DIFF)
        k = proj[..., o + QK_WIDTH:o + 2 * QK_WIDTH].reshape(b, s, 2 * H_DIFF, DH_DIFF)
        v = proj[..., o + 2 * QK_WIDTH:].reshape(b, s, H_DIFF, 2 * DH_DIFF)

        a = a_in[..., :C_A] * jax.nn.sigmoid(a_in[..., C_A:])
        a = causal_depthwise_conv(a, conv_a_w[l], conv_a_b[l])
        a = jax.nn.silu(layer_norm(a, ln_a_g[l], ln_a_b[l]))

        lam_p = diff_lambda[l].astype(jnp.float32)
        lambda_init = 0.8 - 0.6 * math.exp(-0.3 * l)
        lam = (jnp.exp(jnp.sum(lam_p[0] * lam_p[1])) - jnp.exp(jnp.sum(lam_p[2] * lam_p[3]))
               + lambda_init)
        d = diff_attention(q, k, v, lam, bias_dist_t)
        d = rms_norm(d, subln_g[l], SUBLN_EPS) * (1.0 - lambda_init)
        d = d.reshape(b, s, V_WIDTH)

        x = x + jnp.concatenate([a, d], axis=-1) @ w_out[l]

        hc = rms_norm(x, g_cross[l])
        m = rms_norm(mem, g_mem[l])
        cq = (hc @ w_cq[l]).reshape(b, s, H_CROSS, DH_CROSS)
        ckv = (m @ w_ckv[l]).reshape(b, N_MEM, 2, H_CROSS, DH_CROSS)
        ck, cv = ckv[:, :, 0], ckv[:, :, 1]
        cl = jnp.einsum("bqhd,bmhd->bhqm", cq, ck).astype(jnp.float32) * DH_CROSS ** -0.5
        cp = jax.nn.softmax(cl, axis=-1).astype(cv.dtype)
        co = jnp.einsum("bhqm,bmhd->bqhd", cp, cv).reshape(b, s, H_CROSS * DH_CROSS)
        x = x + co @ w_co[l]

        hf = rms_norm(x, g_ffn[l])
        u = causal_depthwise_conv(hf @ w_up[l], conv_f_w[l], conv_f_b[l])
        x = x + (jax.nn.silu(u[..., :D_FF]) * u[..., D_FF:]) @ w_down[l]

    return rms_norm(x, g_final)
```

```python
import functools
import math

import jax
import jax.numpy as jnp
from jax import lax
from jax.experimental import pallas as pl
from jax.experimental.pallas import tpu as pltpu

N_MEM = 256
C_A = 1024
CONV_A_WIDTH = 31
H_DIFF = 8
DH_DIFF = 64
QK_WIDTH = 2 * H_DIFF * DH_DIFF
V_WIDTH = H_DIFF * 2 * DH_DIFF
N_BUCKETS = 32
MAX_DISTANCE = 128
H_CROSS = 4
DH_CROSS = 128
CONV_F_WIDTH = 3
NORM_EPS = 1e-6
SUBLN_EPS = 1e-5
NEG_INF = -1e30

LANES = 128
SUBLANES = 8
BF16_SUBLANES = 16
VMEM_LIMIT_BYTES = 56 * 1024 * 1024

F32 = jnp.float32
BF16 = jnp.bfloat16

HALO_A = 32
HALO_F = BF16_SUBLANES


def _params(*sem):
    return pltpu.CompilerParams(dimension_semantics=sem, vmem_limit_bytes=VMEM_LIMIT_BYTES)


def _rms(x, g, eps):
    return x * lax.rsqrt(jnp.mean(x * x, axis=-1, keepdims=True) + eps) * g


def _dot(a, b):
    return jnp.dot(a, b, preferred_element_type=F32)


def _dot_nt(a, b):
    return lax.dot_general(a, b, (((1,), (1,)), ((), ())), preferred_element_type=F32)


def _pick_tile(n, target):
    t = min(n, target)
    while n % t:
        t //= 2
    return t


def _bias_tiles_kernel(table_ref, o_ref, *, t):
    qi = lax.broadcasted_iota(jnp.int32, (t, t), 0)
    ki = lax.broadcasted_iota(jnp.int32, (t, t), 1)
    max_exact = N_BUCKETS // 2
    for j in range(2):
        n = qi - ki + j * t
        nf = jnp.maximum(n, 1).astype(F32)
        large = max_exact + (jnp.log(nf / max_exact) / math.log(MAX_DISTANCE / max_exact)
                             * (N_BUCKETS - max_exact)).astype(jnp.int32)
        large = jnp.minimum(large, N_BUCKETS - 1)
        bucket = jnp.where(n < max_exact, n, large)
        for h in range(H_DIFF):
            far = table_ref[N_BUCKETS - 1, h]
            b = jnp.zeros((t, t), F32)
            for k in range(N_BUCKETS - 1):
                b = jnp.where(bucket == k, table_ref[k, h] - far, b)
            if j == 0:
                b = jnp.where(n >= 0, b, NEG_INF)
            o_ref[h, j, 0:t, :] = b
            o_ref[h, j, t:, :] = b


def _bias_tiles(table, t):
    return pl.pallas_call(
        functools.partial(_bias_tiles_kernel, t=t),
        out_shape=jax.ShapeDtypeStruct((H_DIFF, 2, 2 * t, t), F32),
        in_specs=[pl.BlockSpec(memory_space=pltpu.SMEM)],
        out_specs=pl.BlockSpec(memory_space=pltpu.VMEM),
        compiler_params=pltpu.CompilerParams(vmem_limit_bytes=VMEM_LIMIT_BYTES),
        name="bias_tiles",
    )(table)


def _mem_kv_kernel(mem_ref, g_ref, w_ref, o_ref):
    m = _rms(mem_ref[...], g_ref[...], NORM_EPS).astype(BF16)
    o_ref[...] = _dot(m, w_ref[...]).astype(o_ref.dtype)


def _mem_kv(mem, g_mem, w_ckv):
    depth, d_model, n_out = w_ckv.shape
    n_mem = mem.shape[0]
    return pl.pallas_call(
        _mem_kv_kernel,
        out_shape=jax.ShapeDtypeStruct((depth, n_mem, n_out), BF16),
        grid=(depth,),
        in_specs=[
            pl.BlockSpec((n_mem, d_model), lambda l: (0, 0)),
            pl.BlockSpec((None, 1, d_model), lambda l: (l, 0, 0)),
            pl.BlockSpec((None, d_model, n_out), lambda l: (l, 0, 0)),
        ],
        out_specs=pl.BlockSpec((None, n_mem, n_out), lambda l: (l, 0, 0)),
        compiler_params=_params("arbitrary"),
        name="mem_kv",
    )(mem, g_mem, w_ckv)


def _in_proj_kernel(x_ref, g_ref, w_ref, o_ref, h_ref):
    @pl.when(pl.program_id(1) == 0)
    def _():
        h_ref[...] = _rms(x_ref[...], g_ref[...], NORM_EPS).astype(h_ref.dtype)

    o_ref[...] = _dot(h_ref[...], w_ref[...]).astype(o_ref.dtype)


def _in_proj(x, g, w, *, tm, tn):
    s, d_model = x.shape
    n = w.shape[1]
    return pl.pallas_call(
        _in_proj_kernel,
        out_shape=jax.ShapeDtypeStruct((s, n), BF16),
        grid=(s // tm, n // tn),
        in_specs=[
            pl.BlockSpec((tm, d_model), lambda i, j: (i, 0)),
            pl.BlockSpec((1, d_model), lambda i, j: (0, 0)),
            pl.BlockSpec((d_model, tn), lambda i, j: (0, j)),
        ],
        out_specs=pl.BlockSpec((tm, tn), lambda i, j: (i, j)),
        scratch_shapes=[pltpu.VMEM((tm, d_model), BF16)],
        compiler_params=_params("arbitrary", "arbitrary"),
        name="in_proj",
    )(x, g, w)


def _conv_module_kernel(main_ref, halo_ref, cw_ref, cb_ref, lg_ref, lb_ref, o_ref, g_ref, sh_ref, *, ts, rows):
    def glu(v):
        v = v.astype(F32)
        return v[:, :C_A] * jax.nn.sigmoid(v[:, C_A:])

    g_ref[0:HALO_A, :] = jnp.where(pl.program_id(0) > 0, glu(halo_ref[...]), 0.0)
    g_ref[HALO_A:, :] = glu(main_ref[...])
    sh_rows = sh_ref.shape[1]
    for k in range(1, SUBLANES):
        sh_ref[k - 1] = g_ref[k:k + sh_rows, :]

    first_tap = HALO_A - (CONV_A_WIDTH - 1)

    def chunk(r, carry):
        base = pl.multiple_of(r * rows, rows)
        acc = jnp.broadcast_to(cb_ref[...], (rows, C_A))
        for j in range(CONV_A_WIDTH):
            k = (first_tap + j) % SUBLANES
            src = g_ref if k == 0 else sh_ref.at[k - 1]
            acc = acc + cw_ref[j:j + 1, :] * src[pl.ds(base + first_tap + j - k, rows), :]
        mu = jnp.mean(acc, axis=-1, keepdims=True)
        xc = acc - mu
        y = xc * lax.rsqrt(jnp.mean(xc * xc, axis=-1, keepdims=True) + NORM_EPS)
        y = y * lg_ref[...] + lb_ref[...]
        o_ref[pl.ds(base, rows), :] = (y * jax.nn.sigmoid(y)).astype(o_ref.dtype)
        return carry

    lax.fori_loop(0, ts // rows, chunk, 0)


def _conv_module(proj, cw, cb, lg, lb, *, ts, rows=BF16_SUBLANES):
    s = proj.shape[0]
    halo_blocks = ts // HALO_A
    return pl.pallas_call(
        functools.partial(_conv_module_kernel, ts=ts, rows=rows),
        out_shape=jax.ShapeDtypeStruct((s, C_A), BF16),
        grid=(s // ts,),
        in_specs=[
            pl.BlockSpec((ts, 2 * C_A), lambda i: (i, 0)),
            pl.BlockSpec((HALO_A, 2 * C_A), lambda i: (jnp.maximum(i * halo_blocks - 1, 0), 0)),
            pl.BlockSpec((CONV_A_WIDTH, C_A), lambda i: (0, 0)),
            pl.BlockSpec((1, C_A), lambda i: (0, 0)),
            pl.BlockSpec((1, C_A), lambda i: (0, 0)),
            pl.BlockSpec((1, C_A), lambda i: (0, 0)),
        ],
        out_specs=pl.BlockSpec((ts, C_A), lambda i: (i, 0)),
        scratch_shapes=[pltpu.VMEM((ts + HALO_A, C_A), F32),
                        pltpu.VMEM((SUBLANES - 1, ts + HALO_A - SUBLANES, C_A), F32)],
        compiler_params=_params("arbitrary"),
        name="conv_module",
    )(proj, proj, cw, cb, lg, lb)


def _diff_attn_kernel(q_ref, k_ref, v_ref, bias_ref, lam_ref, g_ref, o_ref,
                      qs_ref, m_ref, l_ref, acc_ref, *, t, lambda_init):
    qi = pl.program_id(1)

    q = q_ref[...] * (DH_DIFF ** -0.5)
    lane = lax.broadcasted_iota(jnp.int32, q.shape, 1)
    zero = jnp.zeros_like(q)
    qs_ref[0:t, :] = jnp.where(lane < DH_DIFF, q, zero)
    qs_ref[t:, :] = jnp.where(lane >= DH_DIFF, q, zero)
    m_ref[...] = jnp.full(m_ref.shape, NEG_INF, F32)
    l_ref[...] = jnp.zeros(l_ref.shape, F32)
    acc_ref[...] = jnp.zeros(acc_ref.shape, F32)

    def step(kb, bias):
        start = pl.multiple_of(kb * t, t)
        s = _dot_nt(qs_ref[...], k_ref[pl.ds(start, t), :])
        if bias is not None:
            s = s + bias
        m_prev = m_ref[...]
        m_new = jnp.maximum(m_prev, jnp.max(s, axis=-1, keepdims=True))
        alpha = jnp.exp(m_prev - m_new)
        p = jnp.exp(s - m_new)
        l_ref[...] = alpha * l_ref[...] + jnp.sum(p, axis=-1, keepdims=True)
        acc_ref[...] = alpha * acc_ref[...] + _dot(p.astype(BF16), v_ref[pl.ds(start, t), :])
        m_ref[...] = m_new

    def far_step(kb, carry):
        step(kb, None)
        return carry

    lax.fori_loop(0, jnp.maximum(qi - 1, 0), far_step, 0)

    @pl.when(qi >= 1)
    def _():
        step(qi - 1, bias_ref[1])

    step(qi, bias_ref[0])

    lam_p = lam_ref[...]
    lam = (jnp.exp(jnp.sum(lam_p[0:1] * lam_p[1:2], axis=-1, keepdims=True))
           - jnp.exp(jnp.sum(lam_p[2:3] * lam_p[3:4], axis=-1, keepdims=True)) + lambda_init)
    o = acc_ref[...] / l_ref[...]
    d = o[0:t] - lam * o[t:]
    d = _rms(d, g_ref[...], SUBLN_EPS) * (1.0 - lambda_init)
    o_ref[...] = d.astype(o_ref.dtype)


def _diff_attn(proj, bias_tiles, lam_p, g, *, t, lambda_init):
    s = proj.shape[0]
    dv = 2 * DH_DIFF
    q_col = 2 * C_A // dv
    k_col = q_col + QK_WIDTH // dv
    v_col = k_col + QK_WIDTH // dv
    return pl.pallas_call(
        functools.partial(_diff_attn_kernel, t=t, lambda_init=lambda_init),
        out_shape=jax.ShapeDtypeStruct((s, V_WIDTH), BF16),
        grid=(H_DIFF, s // t),
        in_specs=[
            pl.BlockSpec((t, dv), lambda h, i: (i, q_col + h)),
            pl.BlockSpec((s, dv), lambda h, i: (0, k_col + h)),
            pl.BlockSpec((s, dv), lambda h, i: (0, v_col + h)),
            pl.BlockSpec((None, 2, 2 * t, t), lambda h, i: (h, 0, 0, 0)),
            pl.BlockSpec((4, DH_DIFF), lambda h, i: (0, 0)),
            pl.BlockSpec((1, dv), lambda h, i: (0, 0)),
        ],
        out_specs=pl.BlockSpec((t, dv), lambda h, i: (i, h)),
        scratch_shapes=[
            pltpu.VMEM((2 * t, dv), BF16),
            pltpu.VMEM((2 * t, 1), F32),
            pltpu.VMEM((2 * t, 1), F32),
            pltpu.VMEM((2 * t, dv), F32),
        ],
        compiler_params=_params("arbitrary", "arbitrary"),
        name="diff_attn",
    )(proj, proj, proj, bias_tiles, lam_p, g)


def _mix_cross_kernel(a_ref, d_ref, x_ref, wo_ref, gc_ref, wcq_ref, ckv_ref, wco_ref, gf_ref,
                      xo_ref, hf_ref):
    x1 = x_ref[...] + _dot(a_ref[...], wo_ref[0:C_A, :]) + _dot(d_ref[...], wo_ref[C_A:, :])
    hc = _rms(x1, gc_ref[...], NORM_EPS).astype(BF16)
    cq = _dot(hc, wcq_ref[...]).astype(BF16)
    kv_off = H_CROSS * DH_CROSS
    heads = []
    for h in range(H_CROSS):
        lo, hi = h * DH_CROSS, (h + 1) * DH_CROSS
        cl = _dot_nt(cq[:, lo:hi], ckv_ref[:, lo:hi]) * (DH_CROSS ** -0.5)
        e = jnp.exp(cl - jnp.max(cl, axis=-1, keepdims=True))
        l = jnp.sum(e, axis=-1, keepdims=True)
        heads.append((_dot(e.astype(BF16), ckv_ref[:, kv_off + lo:kv_off + hi]) / l).astype(BF16))
    x2 = x1 + _dot(jnp.concatenate(heads, axis=1), wco_ref[...])
    xo_ref[...] = x2
    hf_ref[...] = _rms(x2, gf_ref[...], NORM_EPS).astype(hf_ref.dtype)


def _mix_cross(a, d, x, w_out, g_cross, w_cq, ckv, w_co, g_ffn, *, tm):
    s, d_model = x.shape
    n_cq = w_cq.shape[1]
    const = lambda i: (0, 0)
    return pl.pallas_call(
        _mix_cross_kernel,
        out_shape=(jax.ShapeDtypeStruct((s, d_model), F32), jax.ShapeDtypeStruct((s, d_model), BF16)),
        grid=(s // tm,),
        in_specs=[
            pl.BlockSpec((tm, C_A), lambda i: (i, 0)),
            pl.BlockSpec((tm, V_WIDTH), lambda i: (i, 0)),
            pl.BlockSpec((tm, d_model), lambda i: (i, 0)),
            pl.BlockSpec((C_A + V_WIDTH, d_model), const),
            pl.BlockSpec((1, d_model), const),
            pl.BlockSpec((d_model, n_cq), const),
            pl.BlockSpec(ckv.shape, const),
            pl.BlockSpec((n_cq, d_model), const),
            pl.BlockSpec((1, d_model), const),
        ],
        out_specs=(pl.BlockSpec((tm, d_model), lambda i: (i, 0)),
                   pl.BlockSpec((tm, d_model), lambda i: (i, 0))),
        input_output_aliases={2: 0},
        compiler_params=_params("arbitrary"),
        name="mix_cross",
    )(a, d, x, w_out, g_cross, w_cq, ckv, w_co, g_ffn)


def _ffn_kernel(hf_ref, halo_ref, x_ref, wg_ref, wv_ref, cwg_ref, cwv_ref, cbg_ref, cbv_ref, wd_ref,
                o_ref, ext_ref, ug_ref, uv_ref, *, tm):
    @pl.when(pl.program_id(1) == 0)
    def _():
        halo = halo_ref[...]
        ext_ref[0:HALO_F, :] = jnp.where(pl.program_id(0) > 0, halo, jnp.zeros_like(halo))
        ext_ref[HALO_F:, :] = hf_ref[...]
        o_ref[...] = x_ref[...]

    ext = ext_ref[...]
    ug_ref[...] = _dot(ext, wg_ref[...])
    uv_ref[...] = _dot(ext, wv_ref[...])

    def conv(u_ref, cw_ref, cb_ref):
        out = cb_ref[...]
        for j in range(CONV_F_WIDTH):
            out = out + cw_ref[j:j + 1, :] * u_ref[pl.ds(HALO_F - (CONV_F_WIDTH - 1) + j, tm), :]
        return out

    y = jax.nn.silu(conv(ug_ref, cwg_ref, cbg_ref)) * conv(uv_ref, cwv_ref, cbv_ref)
    o_ref[...] += _dot(y.astype(BF16), wd_ref[...])


def _ffn(hf, x, w_up, cw, cb, w_down, *, tm, tf):
    s, d_model = x.shape
    d_ff = w_down.shape[0]
    nf = d_ff // tf
    halo_blocks = tm // HALO_F
    return pl.pallas_call(
        functools.partial(_ffn_kernel, tm=tm),
        out_shape=jax.ShapeDtypeStruct((s, d_model), F32),
        grid=(s // tm, nf),
        in_specs=[
            pl.BlockSpec((tm, d_model), lambda i, f: (i, 0)),
            pl.BlockSpec((HALO_F, d_model), lambda i, f: (jnp.maximum(i * halo_blocks - 1, 0), 0)),
            pl.BlockSpec((tm, d_model), lambda i, f: (i, 0)),
            pl.BlockSpec((d_model, tf), lambda i, f: (0, f)),
            pl.BlockSpec((d_model, tf), lambda i, f: (0, nf + f)),
            pl.BlockSpec((CONV_F_WIDTH, tf), lambda i, f: (0, f)),
            pl.BlockSpec((CONV_F_WIDTH, tf), lambda i, f: (0, nf + f)),
            pl.BlockSpec((1, tf), lambda i, f: (0, f)),
            pl.BlockSpec((1, tf), lambda i, f: (0, nf + f)),
            pl.BlockSpec((tf, d_model), lambda i, f: (f, 0)),
        ],
        out_specs=pl.BlockSpec((tm, d_model), lambda i, f: (i, 0)),
        scratch_shapes=[
            pltpu.VMEM((tm + HALO_F, d_model), BF16),
            pltpu.VMEM((tm + HALO_F, tf), F32),
            pltpu.VMEM((tm + HALO_F, tf), F32),
        ],
        input_output_aliases={2: 0},
        compiler_params=_params("arbitrary", "arbitrary"),
        name="conv_ffn",
    )(hf, hf, x, w_up, w_up, cw, cw, cb, cb, w_down)


def _final_norm_kernel(x_ref, g_ref, o_ref):
    o_ref[...] = _rms(x_ref[...], g_ref[...], NORM_EPS)


def _final_norm(x, g, *, tm):
    s, d_model = x.shape
    return pl.pallas_call(
        _final_norm_kernel,
        out_shape=jax.ShapeDtypeStruct((s, d_model), F32),
        grid=(s // tm,),
        in_specs=[pl.BlockSpec((tm, d_model), lambda i: (i, 0)),
                  pl.BlockSpec((1, d_model), lambda i: (0, 0))],
        out_specs=pl.BlockSpec((tm, d_model), lambda i: (i, 0)),
        compiler_params=_params("arbitrary"),
        name="final_norm",
    )(x, g)


def kernel(x, mem, rel_bias_table, g_mix, w_in, conv_a_w, conv_a_b, ln_a_g, ln_a_b, diff_lambda, subln_g, w_out, g_cross, g_mem, w_cq, w_ckv, w_co, g_ffn, w_up, conv_f_w, conv_f_b, w_down, g_final):
    b, s, d_model = x.shape
    assert b == 1 and mem.shape[0] == 1
    depth = w_in.shape[0]
    d_ff = w_down.shape[1]

    t_attn = _pick_tile(s, 256)
    assert t_attn >= MAX_DISTANCE, "the two biased key blocks must cover every bucketed distance"
    tm_proj = _pick_tile(s, 512)
    tn_proj = _pick_tile(w_in.shape[2], 1024)
    ts_conv = _pick_tile(s, 256)
    tm_mix = _pick_tile(s, 256)
    tm_ffn = _pick_tile(s, 512)
    tf_ffn = _pick_tile(d_ff, 512)

    row = lambda p: p.reshape(depth, 1, p.shape[-1])
    w_in_b, w_out_b, w_cq_b = w_in.astype(BF16), w_out.astype(BF16), w_cq.astype(BF16)
    w_ckv_b, w_co_b = w_ckv.astype(BF16), w_co.astype(BF16)
    w_up_b, w_down_b = w_up.astype(BF16), w_down.astype(BF16)
    g_mix, g_cross, g_ffn, g_mem = row(g_mix), row(g_cross), row(g_ffn), row(g_mem)
    conv_a_b, ln_a_g, ln_a_b = row(conv_a_b), row(ln_a_g), row(ln_a_b)
    subln_g, conv_f_b = row(subln_g), row(conv_f_b)

    bias_tiles = _bias_tiles(rel_bias_table, t_attn)
    ckv = _mem_kv(mem[0], g_mem, w_ckv_b)

    xs = x[0]
    for l in range(depth):
        lambda_init = 0.8 - 0.6 * math.exp(-0.3 * l)
        proj = _in_proj(xs, g_mix[l], w_in_b[l], tm=tm_proj, tn=tn_proj)
        a = _conv_module(proj, conv_a_w[l], conv_a_b[l], ln_a_g[l], ln_a_b[l], ts=ts_conv)
        d = _diff_attn(proj, bias_tiles, diff_lambda[l], subln_g[l], t=t_attn, lambda_init=lambda_init)
        xs, hf = _mix_cross(a, d, xs, w_out_b[l], g_cross[l], w_cq_b[l], ckv[l], w_co_b[l], g_ffn[l],
                            tm=tm_mix)
        xs = _ffn(hf, xs, w_up_b[l], conv_f_w[l], conv_f_b[l], w_down_b[l], tm=tm_ffn, tf=tf_ffn)
    return _final_norm(xs, g_final.reshape(1, d_model), tm=tm_proj)[None]
```

```python
import functools
import math

import jax
import jax.numpy as jnp
from jax import lax
from jax.experimental import pallas as pl
from jax.experimental.pallas import tpu as pltpu

N_MEM = 256
C_A = 1024
CONV_A_WIDTH = 31
H_DIFF = 8
DH_DIFF = 64
QK_WIDTH = 2 * H_DIFF * DH_DIFF
V_WIDTH = H_DIFF * 2 * DH_DIFF
N_BUCKETS = 32
MAX_DISTANCE = 128
H_CROSS = 4
DH_CROSS = 128
CONV_F_WIDTH = 3
NORM_EPS = 1e-6
SUBLN_EPS = 1e-5
NEG_INF = -1e30

LANES = 128
SUBLANES = 8
BF16_SUBLANES = 16
VMEM_LIMIT_BYTES = 56 * 1024 * 1024

F32 = jnp.float32
BF16 = jnp.bfloat16

HALO_A = 32
HALO_F = BF16_SUBLANES
VT_CHUNK = 256

def _params(*sem):
    return pltpu.CompilerParams(dimension_semantics=sem, vmem_limit_bytes=VMEM_LIMIT_BYTES)


def _rms(x, g, eps):
    return x * lax.rsqrt(jnp.mean(x * x, axis=-1, keepdims=True) + eps) * g


def _dot(a, b):
    return jnp.dot(a, b, preferred_element_type=F32)


def _dot_nt(a, b):
    return lax.dot_general(a, b, (((1,), (1,)), ((), ())), preferred_element_type=F32)


def _pick_tile(n, target):
    t = min(n, target)
    while n % t:
        t //= 2
    return t


def _bias_tiles_kernel(table_ref, o_ref, *, t):
    ki = lax.broadcasted_iota(jnp.int32, (t, t), 0)
    qi = lax.broadcasted_iota(jnp.int32, (t, t), 1)
    max_exact = N_BUCKETS // 2
    for j in range(2):
        n = qi - ki + j * t
        nf = jnp.maximum(n, 1).astype(F32)
        large = max_exact + (jnp.log(nf / max_exact) / math.log(MAX_DISTANCE / max_exact)
                             * (N_BUCKETS - max_exact)).astype(jnp.int32)
        large = jnp.minimum(large, N_BUCKETS - 1)
        bucket = jnp.where(n < max_exact, n, large)
        for h in range(H_DIFF):
            far = table_ref[N_BUCKETS - 1, h]
            b = jnp.zeros((t, t), F32)
            for k in range(N_BUCKETS - 1):
                b = jnp.where(bucket == k, table_ref[k, h] - far, b)
            if j == 0:
                b = jnp.where(n >= 0, b, NEG_INF)
            o_ref[h, j, :, 0:t] = b
            o_ref[h, j, :, t:] = b


def _bias_tiles(table, t):
    return pl.pallas_call(
        functools.partial(_bias_tiles_kernel, t=t),
        out_shape=jax.ShapeDtypeStruct((H_DIFF, 2, t, 2 * t), F32),
        in_specs=[pl.BlockSpec(memory_space=pltpu.SMEM)],
        out_specs=pl.BlockSpec(memory_space=pltpu.VMEM),
        compiler_params=pltpu.CompilerParams(vmem_limit_bytes=VMEM_LIMIT_BYTES),
        name="bias_tiles",
    )(table)


def _mem_kv_kernel(mem_ref, g_ref, w_ref, o_ref):
    m = _rms(mem_ref[...], g_ref[...], NORM_EPS).astype(BF16)
    o_ref[...] = _dot(m, w_ref[...]).astype(o_ref.dtype)


def _mem_kv(mem, g_mem, w_ckv):
    depth, d_model, n_out = w_ckv.shape
    n_mem = mem.shape[0]
    return pl.pallas_call(
        _mem_kv_kernel,
        out_shape=jax.ShapeDtypeStruct((depth, n_mem, n_out), BF16),
        grid=(depth,),
        in_specs=[
            pl.BlockSpec((n_mem, d_model), lambda l: (0, 0)),
            pl.BlockSpec((None, 1, d_model), lambda l: (l, 0, 0)),
            pl.BlockSpec((None, d_model, n_out), lambda l: (l, 0, 0)),
        ],
        out_specs=pl.BlockSpec((None, n_mem, n_out), lambda l: (l, 0, 0)),
        compiler_params=_params("arbitrary"),
        name="mem_kv",
    )(mem, g_mem, w_ckv)


def _in_proj_kernel(x_ref, g_ref, w_ref, o_ref, h_ref):
    @pl.when(pl.program_id(1) == 0)
    def _():
        h_ref[...] = _rms(x_ref[...], g_ref[...], NORM_EPS).astype(h_ref.dtype)

    o_ref[...] = _dot(h_ref[...], w_ref[...]).astype(o_ref.dtype)


def _in_proj(x, g, w, *, tm, tn):
    s, d_model = x.shape
    n = w.shape[1]
    return pl.pallas_call(
        _in_proj_kernel,
        out_shape=jax.ShapeDtypeStruct((s, n), BF16),
        grid=(s // tm, n // tn),
        in_specs=[
            pl.BlockSpec((tm, d_model), lambda i, j: (i, 0)),
            pl.BlockSpec((1, d_model), lambda i, j: (0, 0)),
            pl.BlockSpec((d_model, tn), lambda i, j: (0, j)),
        ],
        out_specs=pl.BlockSpec((tm, tn), lambda i, j: (i, j)),
        scratch_shapes=[pltpu.VMEM((tm, d_model), BF16)],
        compiler_params=_params("arbitrary", "arbitrary"),
        name="in_proj",
    )(x, g, w)


def _conv_module_kernel(main_ref, halo_ref, cw_ref, cb_ref, lg_ref, lb_ref, o_ref, g_ref, sh_ref, *, ts, rows):
    def glu(v):
        v = v.astype(F32)
        return v[:, :C_A] * jax.nn.sigmoid(v[:, C_A:])

    g_ref[0:HALO_A, :] = jnp.where(pl.program_id(0) > 0, glu(halo_ref[...]), 0.0)
    g_ref[HALO_A:, :] = glu(main_ref[...])
    sh_rows = sh_ref.shape[1]
    for k in range(1, SUBLANES):
        sh_ref[k - 1] = g_ref[k:k + sh_rows, :]

    first_tap = HALO_A - (CONV_A_WIDTH - 1)

    def chunk(r, carry):
        base = pl.multiple_of(r * rows, rows)
        acc = jnp.broadcast_to(cb_ref[...], (rows, C_A))
        for j in range(CONV_A_WIDTH):
            k = (first_tap + j) % SUBLANES
            src = g_ref if k == 0 else sh_ref.at[k - 1]
            acc = acc + cw_ref[j:j + 1, :] * src[pl.ds(base + first_tap + j - k, rows), :]
        mu = jnp.mean(acc, axis=-1, keepdims=True)
        xc = acc - mu
        y = xc * lax.rsqrt(jnp.mean(xc * xc, axis=-1, keepdims=True) + NORM_EPS)
        y = y * lg_ref[...] + lb_ref[...]
        o_ref[pl.ds(base, rows), :] = (y * jax.nn.sigmoid(y)).astype(o_ref.dtype)
        return carry

    lax.fori_loop(0, ts // rows, chunk, 0)


def _conv_module(proj, cw, cb, lg, lb, *, ts, rows=BF16_SUBLANES):
    s = proj.shape[0]
    halo_blocks = ts // HALO_A
    return pl.pallas_call(
        functools.partial(_conv_module_kernel, ts=ts, rows=rows),
        out_shape=jax.ShapeDtypeStruct((s, C_A), BF16),
        grid=(s // ts,),
        in_specs=[
            pl.BlockSpec((ts, 2 * C_A), lambda i: (i, 0)),
            pl.BlockSpec((HALO_A, 2 * C_A), lambda i: (jnp.maximum(i * halo_blocks - 1, 0), 0)),
            pl.BlockSpec((CONV_A_WIDTH, C_A), lambda i: (0, 0)),
            pl.BlockSpec((1, C_A), lambda i: (0, 0)),
            pl.BlockSpec((1, C_A), lambda i: (0, 0)),
            pl.BlockSpec((1, C_A), lambda i: (0, 0)),
        ],
        out_specs=pl.BlockSpec((ts, C_A), lambda i: (i, 0)),
        scratch_shapes=[pltpu.VMEM((ts + HALO_A, C_A), F32),
                        pltpu.VMEM((SUBLANES - 1, ts + HALO_A - SUBLANES, C_A), F32)],
        compiler_params=_params("arbitrary"),
        name="conv_module",
    )(proj, proj, cw, cb, lg, lb)


def _diff_attn_kernel(q_ref, k_ref, v_ref, bias_ref, lam_ref, g_ref, o_ref,
                      qs_ref, vt_ref, m_ref, l_ref, acc_ref, *, t, lambda_init):
    qi = pl.program_id(1)
    s_len = v_ref.shape[0]

    @pl.when(qi == 0)
    def _():
        for c in range(s_len // VT_CHUNK):
            vt_ref[:, c * VT_CHUNK:(c + 1) * VT_CHUNK] = v_ref[c * VT_CHUNK:(c + 1) * VT_CHUNK, :].T

    q = q_ref[...] * (DH_DIFF ** -0.5)
    lane = lax.broadcasted_iota(jnp.int32, q.shape, 1)
    zero = jnp.zeros_like(q)
    qs_ref[0:t, :] = jnp.where(lane < DH_DIFF, q, zero)
    qs_ref[t:, :] = jnp.where(lane >= DH_DIFF, q, zero)
    m_ref[...] = jnp.full(m_ref.shape, NEG_INF, F32)
    l_ref[...] = jnp.zeros(l_ref.shape, F32)
    acc_ref[...] = jnp.zeros(acc_ref.shape, F32)

    def step(kb, bias_ref_j):
        start = pl.multiple_of(kb * t, t)
        s = _dot_nt(k_ref[pl.ds(start, t), :], qs_ref[...])
        if bias_ref_j is not None:
            s = s + bias_ref_j[...]
        m_prev = m_ref[...]
        m_new = jnp.maximum(m_prev, jnp.max(s, axis=0, keepdims=True))
        alpha = jnp.exp(m_prev - m_new)
        p = jnp.exp(s - m_new)
        l_ref[...] = alpha * l_ref[...] + jnp.sum(p, axis=0, keepdims=True)
        acc_ref[...] = alpha * acc_ref[...] + _dot(vt_ref[:, pl.ds(start, t)], p.astype(BF16))
        m_ref[...] = m_new

    def far_step(kb, carry):
        step(kb, None)
        return carry

    lax.fori_loop(0, jnp.maximum(qi - 1, 0), far_step, 0)

    @pl.when(qi >= 1)
    def _():
        step(qi - 1, bias_ref.at[1])

    step(qi, bias_ref.at[0])

    lam_p = lam_ref[...]
    lam = (jnp.exp(jnp.sum(lam_p[0:1] * lam_p[1:2], axis=-1, keepdims=True))
           - jnp.exp(jnp.sum(lam_p[2:3] * lam_p[3:4], axis=-1, keepdims=True)) + lambda_init)
    o = acc_ref[...] / l_ref[...]
    d = o[:, 0:t] - lam * o[:, t:]
    d = d * lax.rsqrt(jnp.mean(d * d, axis=0, keepdims=True) + SUBLN_EPS)
    o_ref[...] = (d.T * g_ref[...] * (1.0 - lambda_init)).astype(o_ref.dtype)


def _diff_attn(proj, bias_tiles, lam_p, g, *, t, lambda_init):
    s = proj.shape[0]
    dv = 2 * DH_DIFF
    q_col = 2 * C_A // dv
    k_col = q_col + QK_WIDTH // dv
    v_col = k_col + QK_WIDTH // dv
    return pl.pallas_call(
        functools.partial(_diff_attn_kernel, t=t, lambda_init=lambda_init),
        out_shape=jax.ShapeDtypeStruct((s, V_WIDTH), BF16),
        grid=(H_DIFF, s // t),
        in_specs=[
            pl.BlockSpec((t, dv), lambda h, i: (i, q_col + h)),
            pl.BlockSpec((s, dv), lambda h, i: (0, k_col + h)),
            pl.BlockSpec((s, dv), lambda h, i: (0, v_col + h)),
            pl.BlockSpec((None, 2, t, 2 * t), lambda h, i: (h, 0, 0, 0)),
            pl.BlockSpec((4, DH_DIFF), lambda h, i: (0, 0)),
            pl.BlockSpec((1, dv), lambda h, i: (0, 0)),
        ],
        out_specs=pl.BlockSpec((t, dv), lambda h, i: (i, h)),
        scratch_shapes=[
            pltpu.VMEM((2 * t, dv), BF16),
            pltpu.VMEM((dv, s), BF16),
            pltpu.VMEM((1, 2 * t), F32),
            pltpu.VMEM((1, 2 * t), F32),
            pltpu.VMEM((dv, 2 * t), F32),
        ],
        compiler_params=_params("arbitrary", "arbitrary"),
        name="diff_attn",
    )(proj, proj, proj, bias_tiles, lam_p, g)


def _mix_cross_kernel(a_ref, d_ref, x_ref, wo_ref, gc_ref, wcq_ref, ckv_ref, wco_ref, gf_ref,
                      xo_ref, hf_ref):
    x1 = x_ref[...] + _dot(a_ref[...], wo_ref[0:C_A, :]) + _dot(d_ref[...], wo_ref[C_A:, :])
    hc = _rms(x1, gc_ref[...], NORM_EPS).astype(BF16)
    cq = _dot(hc, wcq_ref[...]).astype(BF16)
    kv_off = H_CROSS * DH_CROSS
    heads = []
    for h in range(H_CROSS):
        lo, hi = h * DH_CROSS, (h + 1) * DH_CROSS
        cl = _dot_nt(cq[:, lo:hi], ckv_ref[:, lo:hi]) * (DH_CROSS ** -0.5)
        e = jnp.exp(cl - jnp.max(cl, axis=-1, keepdims=True))
        l = jnp.sum(e, axis=-1, keepdims=True)
        heads.append((_dot(e.astype(BF16), ckv_ref[:, kv_off + lo:kv_off + hi]) / l).astype(BF16))
    x2 = x1 + _dot(jnp.concatenate(heads, axis=1), wco_ref[...])
    xo_ref[...] = x2
    hf_ref[...] = _rms(x2, gf_ref[...], NORM_EPS).astype(hf_ref.dtype)


def _mix_cross(a, d, x, w_out, g_cross, w_cq, ckv, w_co, g_ffn, *, tm):
    s, d_model = x.shape
    n_cq = w_cq.shape[1]
    const = lambda i: (0, 0)
    return pl.pallas_call(
        _mix_cross_kernel,
        out_shape=(jax.ShapeDtypeStruct((s, d_model), F32), jax.ShapeDtypeStruct((s, d_model), BF16)),
        grid=(s // tm,),
        in_specs=[
            pl.BlockSpec((tm, C_A), lambda i: (i, 0)),
            pl.BlockSpec((tm, V_WIDTH), lambda i: (i, 0)),
            pl.BlockSpec((tm, d_model), lambda i: (i, 0)),
            pl.BlockSpec((C_A + V_WIDTH, d_model), const),
            pl.BlockSpec((1, d_model), const),
            pl.BlockSpec((d_model, n_cq), const),
            pl.BlockSpec(ckv.shape, const),
            pl.BlockSpec((n_cq, d_model), const),
            pl.BlockSpec((1, d_model), const),
        ],
        out_specs=(pl.BlockSpec((tm, d_model), lambda i: (i, 0)),
                   pl.BlockSpec((tm, d_model), lambda i: (i, 0))),
        input_output_aliases={2: 0},
        compiler_params=_params("arbitrary"),
        name="mix_cross",
    )(a, d, x, w_out, g_cross, w_cq, ckv, w_co, g_ffn)


def _ffn_kernel(hf_ref, halo_ref, x_ref, wg_ref, wv_ref, cwg_ref, cwv_ref, cbg_ref, cbv_ref, wd_ref,
                o_ref, ext_ref, ug_ref, uv_ref, *, tm):
    @pl.when(pl.program_id(1) == 0)
    def _():
        halo = halo_ref[...]
        ext_ref[0:HALO_F, :] = jnp.where(pl.program_id(0) > 0, halo, jnp.zeros_like(halo))
        ext_ref[HALO_F:, :] = hf_ref[...]
        o_ref[...] = x_ref[...]

    ext = ext_ref[...]
    ug_ref[...] = _dot(ext, wg_ref[...])
    uv_ref[...] = _dot(ext, wv_ref[...])

    def conv(u_ref, cw_ref, cb_ref):
        out = cb_ref[...]
        for j in range(CONV_F_WIDTH):
            out = out + cw_ref[j:j + 1, :] * u_ref[pl.ds(HALO_F - (CONV_F_WIDTH - 1) + j, tm), :]
        return out

    y = jax.nn.silu(conv(ug_ref, cwg_ref, cbg_ref)) * conv(uv_ref, cwv_ref, cbv_ref)
    o_ref[...] += _dot(y.astype(BF16), wd_ref[...])


def _ffn(hf, x, w_up, cw, cb, w_down, *, tm, tf):
    s, d_model = x.shape
    d_ff = w_down.shape[0]
    nf = d_ff // tf
    halo_blocks = tm // HALO_F
    return pl.pallas_call(
        functools.partial(_ffn_kernel, tm=tm),
        out_shape=jax.ShapeDtypeStruct((s, d_model), F32),
        grid=(s // tm, nf),
        in_specs=[
            pl.BlockSpec((tm, d_model), lambda i, f: (i, 0)),
            pl.BlockSpec((HALO_F, d_model), lambda i, f: (jnp.maximum(i * halo_blocks - 1, 0), 0)),
            pl.BlockSpec((tm, d_model), lambda i, f: (i, 0)),
            pl.BlockSpec((d_model, tf), lambda i, f: (0, f)),
            pl.BlockSpec((d_model, tf), lambda i, f: (0, nf + f)),
            pl.BlockSpec((CONV_F_WIDTH, tf), lambda i, f: (0, f)),
            pl.BlockSpec((CONV_F_WIDTH, tf), lambda i, f: (0, nf + f)),
            pl.BlockSpec((1, tf), lambda i, f: (0, f)),
            pl.BlockSpec((1, tf), lambda i, f: (0, nf + f)),
            pl.BlockSpec((tf, d_model), lambda i, f: (f, 0)),
        ],
        out_specs=pl.BlockSpec((tm, d_model), lambda i, f: (i, 0)),
        scratch_shapes=[
            pltpu.VMEM((tm + HALO_F, d_model), BF16),
            pltpu.VMEM((tm + HALO_F, tf), F32),
            pltpu.VMEM((tm + HALO_F, tf), F32),
        ],
        input_output_aliases={2: 0},
        compiler_params=_params("arbitrary", "arbitrary"),
        name="conv_ffn",
    )(hf, hf, x, w_up, w_up, cw, cw, cb, cb, w_down)


def _final_norm_kernel(x_ref, g_ref, o_ref):
    o_ref[...] = _rms(x_ref[...], g_ref[...], NORM_EPS)


def _final_norm(x, g, *, tm):
    s, d_model = x.shape
    return pl.pallas_call(
        _final_norm_kernel,
        out_shape=jax.ShapeDtypeStruct((s, d_model), F32),
        grid=(s // tm,),
        in_specs=[pl.BlockSpec((tm, d_model), lambda i: (i, 0)),
                  pl.BlockSpec((1, d_model), lambda i: (0, 0))],
        out_specs=pl.BlockSpec((tm, d_model), lambda i: (i, 0)),
        compiler_params=_params("arbitrary"),
        name="final_norm",
    )(x, g)


def kernel(x, mem, rel_bias_table, g_mix, w_in, conv_a_w, conv_a_b, ln_a_g, ln_a_b, diff_lambda, subln_g, w_out, g_cross, g_mem, w_cq, w_ckv, w_co, g_ffn, w_up, conv_f_w, conv_f_b, w_down, g_final):
    b, s, d_model = x.shape
    assert b == 1 and mem.shape[0] == 1
    depth = w_in.shape[0]
    d_ff = w_down.shape[1]

    t_attn = _pick_tile(s, 512)
    assert t_attn >= MAX_DISTANCE, "the two biased key blocks must cover every bucketed distance"
    tm_proj = _pick_tile(s, 512)
    tn_proj = _pick_tile(w_in.shape[2], 1024)
    ts_conv = _pick_tile(s, 256)
    tm_mix = _pick_tile(s, 256)
    tm_ffn = _pick_tile(s, 512)
    tf_ffn = _pick_tile(d_ff, 512)

    row = lambda p: p.reshape(depth, 1, p.shape[-1])
    w_in_b, w_out_b, w_cq_b = w_in.astype(BF16), w_out.astype(BF16), w_cq.astype(BF16)
    w_ckv_b, w_co_b = w_ckv.astype(BF16), w_co.astype(BF16)
    w_up_b, w_down_b = w_up.astype(BF16), w_down.astype(BF16)
    g_mix, g_cross, g_ffn, g_mem = row(g_mix), row(g_cross), row(g_ffn), row(g_mem)
    conv_a_b, ln_a_g, ln_a_b = row(conv_a_b), row(ln_a_g), row(ln_a_b)
    subln_g, conv_f_b = row(subln_g), row(conv_f_b)

    bias_tiles = _bias_tiles(rel_bias_table, t_attn)
    ckv = _mem_kv(mem[0], g_mem, w_ckv_b)

    xs = x[0]
    for l in range(depth):
        lambda_init = 0.8 - 0.6 * math.exp(-0.3 * l)
        proj = _in_proj(xs, g_mix[l], w_in_b[l], tm=tm_proj, tn=tn_proj)
        a = _conv_module(proj, conv_a_w[l], conv_a_b[l], ln_a_g[l], ln_a_b[l], ts=ts_conv)
        d = _diff_attn(proj, bias_tiles, diff_lambda[l], subln_g[l], t=t_attn, lambda_init=lambda_init)
        xs, hf = _mix_cross(a, d, xs, w_out_b[l], g_cross[l], w_cq_b[l], ckv[l], w_co_b[l], g_ffn[l],
                            tm=tm_mix)
        xs = _ffn(hf, xs, w_up_b[l], conv_f_w[l], conv_f_b[l], w_down_b[l], tm=tm_ffn, tf=tf_ffn)
    return _final_norm(xs, g_final.reshape(1, d_model), tm=tm_proj)[None]
```

```python
import functools
import math

import jax
import jax.numpy as jnp
from jax import lax
from jax.experimental import pallas as pl
from jax.experimental.pallas import tpu as pltpu

N_MEM = 256
C_A = 1024
CONV_A_WIDTH = 31
H_DIFF = 8
DH_DIFF = 64
QK_WIDTH = 2 * H_DIFF * DH_DIFF
V_WIDTH = H_DIFF * 2 * DH_DIFF
N_BUCKETS = 32
MAX_DISTANCE = 128
H_CROSS = 4
DH_CROSS = 128
CONV_F_WIDTH = 3
NORM_EPS = 1e-6
SUBLN_EPS = 1e-5
NEG_INF = -1e30

LANES = 128
SUBLANES = 8
BF16_SUBLANES = 16
VMEM_LIMIT_BYTES = 56 * 1024 * 1024

F32 = jnp.float32
BF16 = jnp.bfloat16

HALO_A = 32
HALO_F = BF16_SUBLANES
VT_CHUNK = 256
Q_CHUNK = 256

def _params(*sem):
    return pltpu.CompilerParams(dimension_semantics=sem, vmem_limit_bytes=VMEM_LIMIT_BYTES)


def _rms(x, g, eps):
    return x * lax.rsqrt(jnp.mean(x * x, axis=-1, keepdims=True) + eps) * g


def _dot(a, b):
    return jnp.dot(a, b, preferred_element_type=F32)


def _dot_nt(a, b):
    return lax.dot_general(a, b, (((1,), (1,)), ((), ())), preferred_element_type=F32)


def _pick_tile(n, target):
    t = min(n, target)
    while n % t:
        t //= 2
    return t


def _bias_tiles_kernel(table_ref, o_ref, *, t):
    ki = lax.broadcasted_iota(jnp.int32, (t, t), 0)
    qi = lax.broadcasted_iota(jnp.int32, (t, t), 1)
    max_exact = N_BUCKETS // 2
    for j in range(2):
        n = qi - ki + j * t
        nf = jnp.maximum(n, 1).astype(F32)
        large = max_exact + (jnp.log(nf / max_exact) / math.log(MAX_DISTANCE / max_exact)
                             * (N_BUCKETS - max_exact)).astype(jnp.int32)
        large = jnp.minimum(large, N_BUCKETS - 1)
        bucket = jnp.where(n < max_exact, n, large)
        for h in range(H_DIFF):
            far = table_ref[N_BUCKETS - 1, h]
            b = jnp.zeros((t, t), F32)
            for k in range(N_BUCKETS - 1):
                b = jnp.where(bucket == k, table_ref[k, h] - far, b)
            if j == 0:
                b = jnp.where(n >= 0, b, NEG_INF)
            o_ref[h, j, :, 0:t] = b
            o_ref[h, j, :, t:] = b


def _bias_tiles(table, t):
    return pl.pallas_call(
        functools.partial(_bias_tiles_kernel, t=t),
        out_shape=jax.ShapeDtypeStruct((H_DIFF, 2, t, 2 * t), F32),
        in_specs=[pl.BlockSpec(memory_space=pltpu.SMEM)],
        out_specs=pl.BlockSpec(memory_space=pltpu.VMEM),
        compiler_params=pltpu.CompilerParams(vmem_limit_bytes=VMEM_LIMIT_BYTES),
        name="bias_tiles",
    )(table)


def _mem_kv_kernel(mem_ref, g_ref, w_ref, o_ref):
    m = _rms(mem_ref[...], g_ref[...], NORM_EPS).astype(BF16)
    o_ref[...] = _dot(m, w_ref[...]).astype(o_ref.dtype)


def _mem_kv(mem, g_mem, w_ckv):
    depth, d_model, n_out = w_ckv.shape
    n_mem = mem.shape[0]
    return pl.pallas_call(
        _mem_kv_kernel,
        out_shape=jax.ShapeDtypeStruct((depth, n_mem, n_out), BF16),
        grid=(depth,),
        in_specs=[
            pl.BlockSpec((n_mem, d_model), lambda l: (0, 0)),
            pl.BlockSpec((None, 1, d_model), lambda l: (l, 0, 0)),
            pl.BlockSpec((None, d_model, n_out), lambda l: (l, 0, 0)),
        ],
        out_specs=pl.BlockSpec((None, n_mem, n_out), lambda l: (l, 0, 0)),
        compiler_params=_params("arbitrary"),
        name="mem_kv",
    )(mem, g_mem, w_ckv)


def _in_proj_kernel(x_ref, g_ref, w_ref, o_ref, h_ref):
    @pl.when(pl.program_id(1) == 0)
    def _():
        h_ref[...] = _rms(x_ref[...], g_ref[...], NORM_EPS).astype(h_ref.dtype)

    o_ref[...] = _dot(h_ref[...], w_ref[...]).astype(o_ref.dtype)


def _in_proj(x, g, w, *, tm, tn):
    s, d_model = x.shape
    n = w.shape[1]
    return pl.pallas_call(
        _in_proj_kernel,
        out_shape=jax.ShapeDtypeStruct((s, n), BF16),
        grid=(s // tm, n // tn),
        in_specs=[
            pl.BlockSpec((tm, d_model), lambda i, j: (i, 0)),
            pl.BlockSpec((1, d_model), lambda i, j: (0, 0)),
            pl.BlockSpec((d_model, tn), lambda i, j: (0, j)),
        ],
        out_specs=pl.BlockSpec((tm, tn), lambda i, j: (i, j)),
        scratch_shapes=[pltpu.VMEM((tm, d_model), BF16)],
        compiler_params=_params("arbitrary", "arbitrary"),
        name="in_proj",
    )(x, g, w)


def _conv_module_kernel(main_ref, halo_ref, cw_ref, cb_ref, lg_ref, lb_ref, o_ref, g_ref, sh_ref, *, ts, rows):
    def glu(v):
        v = v.astype(F32)
        return v[:, :C_A] * jax.nn.sigmoid(v[:, C_A:])

    g_ref[0:HALO_A, :] = jnp.where(pl.program_id(0) > 0, glu(halo_ref[...]), 0.0)
    g_ref[HALO_A:, :] = glu(main_ref[...])
    sh_rows = sh_ref.shape[1]
    for k in range(1, SUBLANES):
        sh_ref[k - 1] = g_ref[k:k + sh_rows, :]

    first_tap = HALO_A - (CONV_A_WIDTH - 1)

    def chunk(r, carry):
        base = pl.multiple_of(r * rows, rows)
        acc = jnp.broadcast_to(cb_ref[...], (rows, C_A))
        for j in range(CONV_A_WIDTH):
            k = (first_tap + j) % SUBLANES
            src = g_ref if k == 0 else sh_ref.at[k - 1]
            acc = acc + cw_ref[j:j + 1, :] * src[pl.ds(base + first_tap + j - k, rows), :]
        mu = jnp.mean(acc, axis=-1, keepdims=True)
        xc = acc - mu
        y = xc * lax.rsqrt(jnp.mean(xc * xc, axis=-1, keepdims=True) + NORM_EPS)
        y = y * lg_ref[...] + lb_ref[...]
        o_ref[pl.ds(base, rows), :] = (y * jax.nn.sigmoid(y)).astype(o_ref.dtype)
        return carry

    lax.fori_loop(0, ts // rows, chunk, 0)


def _conv_module(proj, cw, cb, lg, lb, *, ts, rows=BF16_SUBLANES):
    s = proj.shape[0]
    halo_blocks = ts // HALO_A
    return pl.pallas_call(
        functools.partial(_conv_module_kernel, ts=ts, rows=rows),
        out_shape=jax.ShapeDtypeStruct((s, C_A), BF16),
        grid=(s // ts,),
        in_specs=[
            pl.BlockSpec((ts, 2 * C_A), lambda i: (i, 0)),
            pl.BlockSpec((HALO_A, 2 * C_A), lambda i: (jnp.maximum(i * halo_blocks - 1, 0), 0)),
            pl.BlockSpec((CONV_A_WIDTH, C_A), lambda i: (0, 0)),
            pl.BlockSpec((1, C_A), lambda i: (0, 0)),
            pl.BlockSpec((1, C_A), lambda i: (0, 0)),
            pl.BlockSpec((1, C_A), lambda i: (0, 0)),
        ],
        out_specs=pl.BlockSpec((ts, C_A), lambda i: (i, 0)),
        scratch_shapes=[pltpu.VMEM((ts + HALO_A, C_A), F32),
                        pltpu.VMEM((SUBLANES - 1, ts + HALO_A - SUBLANES, C_A), F32)],
        compiler_params=_params("arbitrary"),
        name="conv_module",
    )(proj, proj, cw, cb, lg, lb)


def _diff_attn_kernel(q_ref, k_ref, v_ref, bias_ref, lam_ref, g_ref, o_ref,
                      qs_ref, vt_ref, s_ref, m_ref, l_ref, acc_ref, *, t, lambda_init):
    qi = pl.program_id(1)
    s_len = v_ref.shape[0]
    chunks = [slice(c * Q_CHUNK, (c + 1) * Q_CHUNK) for c in range(2 * t // Q_CHUNK)]

    def load_queries(tile):
        q = q_ref[pl.ds(pl.multiple_of(tile * t, t), t), :] * (DH_DIFF ** -0.5)
        lane = lax.broadcasted_iota(jnp.int32, q.shape, 1)
        zero = jnp.zeros_like(q)
        qs_ref[0:t, :] = jnp.where(lane < DH_DIFF, q, zero)
        qs_ref[t:, :] = jnp.where(lane >= DH_DIFF, q, zero)

    def scores(kb, c):
        return _dot_nt(k_ref[pl.ds(pl.multiple_of(kb * t, t), t), :], qs_ref[chunks[c], :])

    @pl.when(qi == 0)
    def _():
        for c in range(s_len // VT_CHUNK):
            vt_ref[:, c * VT_CHUNK:(c + 1) * VT_CHUNK] = v_ref[c * VT_CHUNK:(c + 1) * VT_CHUNK, :].T
        load_queries(0)
        for c in range(len(chunks)):
            s_ref[c] = scores(0, c)

    m_ref[...] = jnp.full(m_ref.shape, NEG_INF, F32)
    l_ref[...] = jnp.zeros(l_ref.shape, F32)
    acc_ref[...] = jnp.zeros(acc_ref.shape, F32)

    def step(kb, bias_ref_j, next_kb):
        vt_blk = vt_ref[:, pl.ds(pl.multiple_of(kb * t, t), t)]
        for c, cols in enumerate(chunks):
            s = s_ref[c]
            s_ref[c] = scores(next_kb, c)
            if bias_ref_j is not None:
                s = s + bias_ref_j[:, cols]
            m_prev = m_ref[:, cols]
            m_new = jnp.maximum(m_prev, jnp.max(s, axis=0, keepdims=True))
            alpha = jnp.exp(m_prev - m_new)
            p = jnp.exp(s - m_new)
            l_ref[:, cols] = alpha * l_ref[:, cols] + jnp.sum(p, axis=0, keepdims=True)
            acc_ref[:, cols] = alpha * acc_ref[:, cols] + _dot(vt_blk, p.astype(BF16))
            m_ref[:, cols] = m_new

    def far_step(kb, carry):
        step(kb, None, kb + 1)
        return carry

    lax.fori_loop(0, jnp.maximum(qi - 1, 0), far_step, 0)

    @pl.when(qi >= 1)
    def _():
        step(qi - 1, bias_ref.at[1], qi)

    load_queries(jnp.minimum(qi + 1, pl.num_programs(1) - 1))
    step(qi, bias_ref.at[0], 0)

    lam_p = lam_ref[...]
    lam = (jnp.exp(jnp.sum(lam_p[0:1] * lam_p[1:2], axis=-1, keepdims=True))
           - jnp.exp(jnp.sum(lam_p[2:3] * lam_p[3:4], axis=-1, keepdims=True)) + lambda_init)
    o = acc_ref[...] / l_ref[...]
    d = o[:, 0:t] - lam * o[:, t:]
    d = d * lax.rsqrt(jnp.mean(d * d, axis=0, keepdims=True) + SUBLN_EPS)
    o_ref[...] = (d.T * g_ref[...] * (1.0 - lambda_init)).astype(o_ref.dtype)


def _diff_attn(proj, bias_tiles, lam_p, g, *, t, lambda_init):
    s = proj.shape[0]
    dv = 2 * DH_DIFF
    q_col = 2 * C_A // dv
    k_col = q_col + QK_WIDTH // dv
    v_col = k_col + QK_WIDTH // dv
    return pl.pallas_call(
        functools.partial(_diff_attn_kernel, t=t, lambda_init=lambda_init),
        out_shape=jax.ShapeDtypeStruct((s, V_WIDTH), BF16),
        grid=(H_DIFF, s // t),
        in_specs=[
            pl.BlockSpec((s, dv), lambda h, i: (0, q_col + h)),
            pl.BlockSpec((s, dv), lambda h, i: (0, k_col + h)),
            pl.BlockSpec((s, dv), lambda h, i: (0, v_col + h)),
            pl.BlockSpec((None, 2, t, 2 * t), lambda h, i: (h, 0, 0, 0)),
            pl.BlockSpec((4, DH_DIFF), lambda h, i: (0, 0)),
            pl.BlockSpec((1, dv), lambda h, i: (0, 0)),
        ],
        out_specs=pl.BlockSpec((t, dv), lambda h, i: (i, h)),
        scratch_shapes=[
            pltpu.VMEM((2 * t, dv), BF16),
            pltpu.VMEM((dv, s), BF16),
            pltpu.VMEM((2 * t // Q_CHUNK, t, Q_CHUNK), F32),
            pltpu.VMEM((1, 2 * t), F32),
            pltpu.VMEM((1, 2 * t), F32),
            pltpu.VMEM((dv, 2 * t), F32),
        ],
        compiler_params=_params("arbitrary", "arbitrary"),
        name="diff_attn",
    )(proj, proj, proj, bias_tiles, lam_p, g)


def _mix_cross_kernel(a_ref, d_ref, x_ref, wo_ref, gc_ref, wcq_ref, ckv_ref, wco_ref, gf_ref,
                      xo_ref, hf_ref):
    x1 = x_ref[...] + _dot(a_ref[...], wo_ref[0:C_A, :]) + _dot(d_ref[...], wo_ref[C_A:, :])
    hc = _rms(x1, gc_ref[...], NORM_EPS).astype(BF16)
    cq = _dot(hc, wcq_ref[...]).astype(BF16)
    kv_off = H_CROSS * DH_CROSS
    heads = []
    for h in range(H_CROSS):
        lo, hi = h * DH_CROSS, (h + 1) * DH_CROSS
        cl = _dot_nt(cq[:, lo:hi], ckv_ref[:, lo:hi]) * (DH_CROSS ** -0.5)
        e = jnp.exp(cl - jnp.max(cl, axis=-1, keepdims=True))
        l = jnp.sum(e, axis=-1, keepdims=True)
        heads.append((_dot(e.astype(BF16), ckv_ref[:, kv_off + lo:kv_off + hi]) / l).astype(BF16))
    x2 = x1 + _dot(jnp.concatenate(heads, axis=1), wco_ref[...])
    xo_ref[...] = x2
    hf_ref[...] = _rms(x2, gf_ref[...], NORM_EPS).astype(hf_ref.dtype)


def _mix_cross(a, d, x, w_out, g_cross, w_cq, ckv, w_co, g_ffn, *, tm):
    s, d_model = x.shape
    n_cq = w_cq.shape[1]
    const = lambda i: (0, 0)
    return pl.pallas_call(
        _mix_cross_kernel,
        out_shape=(jax.ShapeDtypeStruct((s, d_model), F32), jax.ShapeDtypeStruct((s, d_model), BF16)),
        grid=(s // tm,),
        in_specs=[
            pl.BlockSpec((tm, C_A), lambda i: (i, 0)),
            pl.BlockSpec((tm, V_WIDTH), lambda i: (i, 0)),
            pl.BlockSpec((tm, d_model), lambda i: (i, 0)),
            pl.BlockSpec((C_A + V_WIDTH, d_model), const),
            pl.BlockSpec((1, d_model), const),
            pl.BlockSpec((d_model, n_cq), const),
            pl.BlockSpec(ckv.shape, const),
            pl.BlockSpec((n_cq, d_model), const),
            pl.BlockSpec((1, d_model), const),
        ],
        out_specs=(pl.BlockSpec((tm, d_model), lambda i: (i, 0)),
                   pl.BlockSpec((tm, d_model), lambda i: (i, 0))),
        input_output_aliases={2: 0},
        compiler_params=_params("arbitrary"),
        name="mix_cross",
    )(a, d, x, w_out, g_cross, w_cq, ckv, w_co, g_ffn)


def _ffn_kernel(hf_ref, halo_ref, x_ref, wg_ref, wv_ref, cwg_ref, cwv_ref, cbg_ref, cbv_ref, wd_ref,
                o_ref, ext_ref, ug_ref, uv_ref, *, tm):
    @pl.when(pl.program_id(1) == 0)
    def _():
        halo = halo_ref[...]
        ext_ref[0:HALO_F, :] = jnp.where(pl.program_id(0) > 0, halo, jnp.zeros_like(halo))
        ext_ref[HALO_F:, :] = hf_ref[...]
        o_ref[...] = x_ref[...]

    ext = ext_ref[...]
    ug_ref[...] = _dot(ext, wg_ref[...])
    uv_ref[...] = _dot(ext, wv_ref[...])

    def conv(u_ref, cw_ref, cb_ref):
        out = cb_ref[...]
        for j in range(CONV_F_WIDTH):
            out = out + cw_ref[j:j + 1, :] * u_ref[pl.ds(HALO_F - (CONV_F_WIDTH - 1) + j, tm), :]
        return out

    y = jax.nn.silu(conv(ug_ref, cwg_ref, cbg_ref)) * conv(uv_ref, cwv_ref, cbv_ref)
    o_ref[...] += _dot(y.astype(BF16), wd_ref[...])


def _ffn(hf, x, w_up, cw, cb, w_down, *, tm, tf):
    s, d_model = x.shape
    d_ff = w_down.shape[0]
    nf = d_ff // tf
    halo_blocks = tm // HALO_F
    return pl.pallas_call(
        functools.partial(_ffn_kernel, tm=tm),
        out_shape=jax.ShapeDtypeStruct((s, d_model), F32),
        grid=(s // tm, nf),
        in_specs=[
            pl.BlockSpec((tm, d_model), lambda i, f: (i, 0)),
            pl.BlockSpec((HALO_F, d_model), lambda i, f: (jnp.maximum(i * halo_blocks - 1, 0), 0)),
            pl.BlockSpec((tm, d_model), lambda i, f: (i, 0)),
            pl.BlockSpec((d_model, tf), lambda i, f: (0, f)),
            pl.BlockSpec((d_model, tf), lambda i, f: (0, nf + f)),
            pl.BlockSpec((CONV_F_WIDTH, tf), lambda i, f: (0, f)),
            pl.BlockSpec((CONV_F_WIDTH, tf), lambda i, f: (0, nf + f)),
            pl.BlockSpec((1, tf), lambda i, f: (0, f)),
            pl.BlockSpec((1, tf), lambda i, f: (0, nf + f)),
            pl.BlockSpec((tf, d_model), lambda i, f: (f, 0)),
        ],
        out_specs=pl.BlockSpec((tm, d_model), lambda i, f: (i, 0)),
        scratch_shapes=[
            pltpu.VMEM((tm + HALO_F, d_model), BF16),
            pltpu.VMEM((tm + HALO_F, tf), F32),
            pltpu.VMEM((tm + HALO_F, tf), F32),
        ],
        input_output_aliases={2: 0},
        compiler_params=_params("arbitrary", "arbitrary"),
        name="conv_ffn",
    )(hf, hf, x, w_up, w_up, cw, cw, cb, cb, w_down)


def _final_norm_kernel(x_ref, g_ref, o_ref):
    o_ref[...] = _rms(x_ref[...], g_ref[...], NORM_EPS)


def _final_norm(x, g, *, tm):
    s, d_model = x.shape
    return pl.pallas_call(
        _final_norm_kernel,
        out_shape=jax.ShapeDtypeStruct((s, d_model), F32),
        grid=(s // tm,),
        in_specs=[pl.BlockSpec((tm, d_model), lambda i: (i, 0)),
                  pl.BlockSpec((1, d_model), lambda i: (0, 0))],
        out_specs=pl.BlockSpec((tm, d_model), lambda i: (i, 0)),
        compiler_params=_params("arbitrary"),
        name="final_norm",
    )(x, g)


def kernel(x, mem, rel_bias_table, g_mix, w_in, conv_a_w, conv_a_b, ln_a_g, ln_a_b, diff_lambda, subln_g, w_out, g_cross, g_mem, w_cq, w_ckv, w_co, g_ffn, w_up, conv_f_w, conv_f_b, w_down, g_final):
    b, s, d_model = x.shape
    assert b == 1 and mem.shape[0] == 1
    depth = w_in.shape[0]
    d_ff = w_down.shape[1]

    t_attn = _pick_tile(s, 512)
    assert t_attn >= MAX_DISTANCE, "the two biased key blocks must cover every bucketed distance"
    tm_proj = _pick_tile(s, 512)
    tn_proj = _pick_tile(w_in.shape[2], 1024)
    ts_conv = _pick_tile(s, 256)
    tm_mix = _pick_tile(s, 256)
    tm_ffn = _pick_tile(s, 512)
    tf_ffn = _pick_tile(d_ff, 512)

    row = lambda p: p.reshape(depth, 1, p.shape[-1])
    w_in_b, w_out_b, w_cq_b = w_in.astype(BF16), w_out.astype(BF16), w_cq.astype(BF16)
    w_ckv_b, w_co_b = w_ckv.astype(BF16), w_co.astype(BF16)
    w_up_b, w_down_b = w_up.astype(BF16), w_down.astype(BF16)
    g_mix, g_cross, g_ffn, g_mem = row(g_mix), row(g_cross), row(g_ffn), row(g_mem)
    conv_a_b, ln_a_g, ln_a_b = row(conv_a_b), row(ln_a_g), row(ln_a_b)
    subln_g, conv_f_b = row(subln_g), row(conv_f_b)

    bias_tiles = _bias_tiles(rel_bias_table, t_attn)
    ckv = _mem_kv(mem[0], g_mem, w_ckv_b)

    xs = x[0]
    for l in range(depth):
        lambda_init = 0.8 - 0.6 * math.exp(-0.3 * l)
        proj = _in_proj(xs, g_mix[l], w_in_b[l], tm=tm_proj, tn=tn_proj)
        a = _conv_module(proj, conv_a_w[l], conv_a_b[l], ln_a_g[l], ln_a_b[l], ts=ts_conv)
        d = _diff_attn(proj, bias_tiles, diff_lambda[l], subln_g[l], t=t_attn, lambda_init=lambda_init)
        xs, hf = _mix_cross(a, d, xs, w_out_b[l], g_cross[l], w_cq_b[l], ckv[l], w_co_b[l], g_ffn[l],
                            tm=tm_mix)
        xs = _ffn(hf, xs, w_up_b[l], conv_f_w[l], conv_f_b[l], w_down_b[l], tm=tm_ffn, tf=tf_ffn)
    return _final_norm(xs, g_final.reshape(1, d_model), tm=tm_proj)[None]
```

```python
import functools
import math

import jax
import jax.numpy as jnp
from jax import lax
from jax.experimental import pallas as pl
from jax.experimental.pallas import tpu as pltpu

N_MEM = 256
C_A = 1024
CONV_A_WIDTH = 31
H_DIFF = 8
DH_DIFF = 64
QK_WIDTH = 2 * H_DIFF * DH_DIFF
V_WIDTH = H_DIFF * 2 * DH_DIFF
N_BUCKETS = 32
MAX_DISTANCE = 128
H_CROSS = 4
DH_CROSS = 128
CONV_F_WIDTH = 3
NORM_EPS = 1e-6
SUBLN_EPS = 1e-5
NEG_INF = -1e30

LANES = 128
SUBLANES = 8
BF16_SUBLANES = 16
VMEM_LIMIT_BYTES = 56 * 1024 * 1024

F32 = jnp.float32
BF16 = jnp.bfloat16

HALO_A = 32
HALO_F = BF16_SUBLANES
VT_CHUNK = 256
Q_CHUNK = 256
def _params(*sem):
    return pltpu.CompilerParams(dimension_semantics=sem, vmem_limit_bytes=VMEM_LIMIT_BYTES)


def _rms(x, g, eps):
    return x * lax.rsqrt(jnp.mean(x * x, axis=-1, keepdims=True) + eps) * g


def _dot(a, b):
    return jnp.dot(a, b, preferred_element_type=F32)


def _dot_nt(a, b):
    return lax.dot_general(a, b, (((1,), (1,)), ((), ())), preferred_element_type=F32)


def _pick_tile(n, target):
    t = min(n, target)
    while n % t:
        t //= 2
    return t


def _bias_tiles_kernel(table_ref, o_ref, *, t):
    ki = lax.broadcasted_iota(jnp.int32, (t, t), 0)
    qi = lax.broadcasted_iota(jnp.int32, (t, t), 1)
    max_exact = N_BUCKETS // 2
    for j in range(2):
        n = qi - ki + j * t
        nf = jnp.maximum(n, 1).astype(F32)
        large = max_exact + (jnp.log(nf / max_exact) / math.log(MAX_DISTANCE / max_exact)
                             * (N_BUCKETS - max_exact)).astype(jnp.int32)
        large = jnp.minimum(large, N_BUCKETS - 1)
        bucket = jnp.where(n < max_exact, n, large)
        for h in range(H_DIFF):
            far = table_ref[N_BUCKETS - 1, h]
            b = jnp.zeros((t, t), F32)
            for k in range(N_BUCKETS - 1):
                b = jnp.where(bucket == k, table_ref[k, h] - far, b)
            if j == 0:
                b = jnp.where(n >= 0, b, NEG_INF)
            o_ref[h, j, :, 0:t] = b
            o_ref[h, j, :, t:] = b


def _bias_tiles(table, t):
    return pl.pallas_call(
        functools.partial(_bias_tiles_kernel, t=t),
        out_shape=jax.ShapeDtypeStruct((H_DIFF, 2, t, 2 * t), F32),
        in_specs=[pl.BlockSpec(memory_space=pltpu.SMEM)],
        out_specs=pl.BlockSpec(memory_space=pltpu.VMEM),
        compiler_params=pltpu.CompilerParams(vmem_limit_bytes=VMEM_LIMIT_BYTES),
        name="bias_tiles",
    )(table)


def _mem_kv_kernel(mem_ref, g_ref, w_ref, o_ref):
    m = _rms(mem_ref[...], g_ref[...], NORM_EPS).astype(BF16)
    o_ref[...] = _dot(m, w_ref[...]).astype(o_ref.dtype)


def _mem_kv(mem, g_mem, w_ckv):
    depth, d_model, n_out = w_ckv.shape
    n_mem = mem.shape[0]
    return pl.pallas_call(
        _mem_kv_kernel,
        out_shape=jax.ShapeDtypeStruct((depth, n_mem, n_out), BF16),
        grid=(depth,),
        in_specs=[
            pl.BlockSpec((n_mem, d_model), lambda l: (0, 0)),
            pl.BlockSpec((None, 1, d_model), lambda l: (l, 0, 0)),
            pl.BlockSpec((None, d_model, n_out), lambda l: (l, 0, 0)),
        ],
        out_specs=pl.BlockSpec((None, n_mem, n_out), lambda l: (l, 0, 0)),
        compiler_params=_params("arbitrary"),
        name="mem_kv",
    )(mem, g_mem, w_ckv)


def _in_proj_kernel(x_ref, g_ref, w_ref, o_ref, h_ref):
    @pl.when(pl.program_id(1) == 0)
    def _():
        h_ref[...] = _rms(x_ref[...], g_ref[...], NORM_EPS).astype(h_ref.dtype)

    o_ref[...] = _dot(h_ref[...], w_ref[...]).astype(o_ref.dtype)


def _in_proj(x, g, w, *, layer, tm, tn):
    s, d_model = x.shape
    n = w.shape[2]
    return pl.pallas_call(
        _in_proj_kernel,
        out_shape=jax.ShapeDtypeStruct((s, n), BF16),
        grid=(s // tm, n // tn),
        in_specs=[
            pl.BlockSpec((tm, d_model), lambda i, j: (i, 0)),
            pl.BlockSpec((1, d_model), lambda i, j: (0, 0)),
            pl.BlockSpec((None, d_model, tn), lambda i, j: (layer, 0, j)),
        ],
        out_specs=pl.BlockSpec((tm, tn), lambda i, j: (i, j)),
        scratch_shapes=[pltpu.VMEM((tm, d_model), BF16)],
        compiler_params=_params("arbitrary", "arbitrary"),
        name="in_proj",
    )(x, g, w)


def _conv_module_kernel(main_ref, halo_ref, cw_ref, cb_ref, lg_ref, lb_ref, o_ref,
                        g_ref, sh_ref, conv_ref, *, ts, rows):
    def glu(v):
        v = v.astype(F32)
        return v[:, :C_A] * jax.nn.sigmoid(v[:, C_A:])

    g_ref[0:HALO_A, :] = jnp.where(pl.program_id(0) > 0, glu(halo_ref[...]), 0.0)
    g_ref[HALO_A:, :] = glu(main_ref[...])
    sh_rows = sh_ref.shape[1]
    for k in range(1, SUBLANES):
        sh_ref[k - 1] = g_ref[k:k + sh_rows, :]

    first_tap = HALO_A - (CONV_A_WIDTH - 1)
    groups = rows // SUBLANES

    width = C_A // 2
    for cols in (slice(0, width), slice(width, C_A)):
        def chunk(r, carry, cols=cols):
            base = pl.multiple_of(r * rows, rows)
            acc = jnp.broadcast_to(cb_ref[:, cols].reshape(1, 1, width), (groups, SUBLANES, width))
            for j in range(CONV_A_WIDTH):
                k = (first_tap + j) % SUBLANES
                src = g_ref if k == 0 else sh_ref.at[k - 1]
                win = src[pl.ds(base + first_tap + j - k, rows), cols]
                acc = acc + cw_ref[j, :, cols][None] * win.reshape(groups, SUBLANES, width)
            conv_ref[pl.ds(base, rows), cols] = acc.reshape(rows, width)
            return carry

        lax.fori_loop(0, ts // rows, chunk, 0)

    u = conv_ref[...]
    xc = u - jnp.mean(u, axis=-1, keepdims=True)
    y = xc * lax.rsqrt(jnp.mean(xc * xc, axis=-1, keepdims=True) + NORM_EPS)
    y = y * lg_ref[...] + lb_ref[...]
    o_ref[...] = (y * jax.nn.sigmoid(y)).astype(o_ref.dtype)


def _conv_module(proj, cw, cb, lg, lb, *, ts, rows=4 * SUBLANES):
    s = proj.shape[0]
    halo_blocks = ts // HALO_A
    cw = jnp.broadcast_to(cw[:, None, :], (CONV_A_WIDTH, SUBLANES, C_A))
    return pl.pallas_call(
        functools.partial(_conv_module_kernel, ts=ts, rows=rows),
        out_shape=jax.ShapeDtypeStruct((s, C_A), BF16),
        grid=(s // ts,),
        in_specs=[
            pl.BlockSpec((ts, 2 * C_A), lambda i: (i, 0)),
            pl.BlockSpec((HALO_A, 2 * C_A), lambda i: (jnp.maximum(i * halo_blocks - 1, 0), 0)),
            pl.BlockSpec((CONV_A_WIDTH, SUBLANES, C_A), lambda i: (0, 0, 0)),
            pl.BlockSpec((1, C_A), lambda i: (0, 0)),
            pl.BlockSpec((1, C_A), lambda i: (0, 0)),
            pl.BlockSpec((1, C_A), lambda i: (0, 0)),
        ],
        out_specs=pl.BlockSpec((ts, C_A), lambda i: (i, 0)),
        scratch_shapes=[pltpu.VMEM((ts + HALO_A, C_A), F32),
                        pltpu.VMEM((SUBLANES - 1, ts + HALO_A - SUBLANES, C_A), F32),
                        pltpu.VMEM((ts, C_A), F32)],
        compiler_params=_params("arbitrary"),
        name="conv_module",
    )(proj, proj, cw, cb, lg, lb)


def _diff_attn_kernel(q_ref, k_ref, v_ref, bias_ref, lam_ref, g_ref, o_ref,
                      qs_ref, vt_ref, s_ref, m_ref, l_ref, acc_ref, *, t, lambda_init):
    qi = pl.program_id(1)
    s_len = v_ref.shape[0]
    chunks = [slice(c * Q_CHUNK, (c + 1) * Q_CHUNK) for c in range(2 * t // Q_CHUNK)]

    def load_queries(tile):
        q = q_ref[pl.ds(pl.multiple_of(tile * t, t), t), :] * (DH_DIFF ** -0.5)
        lane = lax.broadcasted_iota(jnp.int32, q.shape, 1)
        zero = jnp.zeros_like(q)
        qs_ref[0:t, :] = jnp.where(lane < DH_DIFF, q, zero)
        qs_ref[t:, :] = jnp.where(lane >= DH_DIFF, q, zero)

    def scores(kb, c):
        return _dot_nt(k_ref[pl.ds(pl.multiple_of(kb * t, t), t), :], qs_ref[chunks[c], :])

    @pl.when(qi == 0)
    def _():
        for c in range(s_len // VT_CHUNK):
            vt_ref[:, c * VT_CHUNK:(c + 1) * VT_CHUNK] = v_ref[c * VT_CHUNK:(c + 1) * VT_CHUNK, :].T
        load_queries(0)
        for c in range(len(chunks)):
            s_ref[c] = scores(0, c)

    m_ref[...] = jnp.full(m_ref.shape, NEG_INF, F32)
    l_ref[...] = jnp.zeros(l_ref.shape, F32)
    acc_ref[...] = jnp.zeros(acc_ref.shape, F32)

    def step(kb, bias_ref_j, next_kb):
        vt_blk = vt_ref[:, pl.ds(pl.multiple_of(kb * t, t), t)]
        for c, cols in enumerate(chunks):
            s = s_ref[c]
            s_ref[c] = scores(next_kb, c)
            if bias_ref_j is not None:
                s = s + bias_ref_j[:, cols]
            m_prev = m_ref[:, cols]
            m_new = jnp.maximum(m_prev, jnp.max(s, axis=0, keepdims=True))
            alpha = jnp.exp(m_prev - m_new)
            p = jnp.exp(s - m_new)
            l_ref[:, cols] = alpha * l_ref[:, cols] + jnp.sum(p, axis=0, keepdims=True)
            acc_ref[:, cols] = alpha * acc_ref[:, cols] + _dot(vt_blk, p.astype(BF16))
            m_ref[:, cols] = m_new

    def far_step(kb, carry):
        step(kb, None, kb + 1)
        return carry

    lax.fori_loop(0, jnp.maximum(qi - 1, 0), far_step, 0)

    @pl.when(qi >= 1)
    def _():
        step(qi - 1, bias_ref.at[1], qi)

    load_queries(jnp.minimum(qi + 1, pl.num_programs(1) - 1))
    step(qi, bias_ref.at[0], 0)

    lam_p = lam_ref[...]
    lam = (jnp.exp(jnp.sum(lam_p[0:1] * lam_p[1:2], axis=-1, keepdims=True))
           - jnp.exp(jnp.sum(lam_p[2:3] * lam_p[3:4], axis=-1, keepdims=True)) + lambda_init)
    o = acc_ref[...] / l_ref[...]
    d = o[:, 0:t] - lam * o[:, t:]
    d = d * lax.rsqrt(jnp.mean(d * d, axis=0, keepdims=True) + SUBLN_EPS)
    o_ref[...] = (d.T * g_ref[...] * (1.0 - lambda_init)).astype(o_ref.dtype)


def _diff_attn(proj, bias_tiles, lam_p, g, *, t, lambda_init):
    s = proj.shape[0]
    dv = 2 * DH_DIFF
    q_col = 2 * C_A // dv
    k_col = q_col + QK_WIDTH // dv
    v_col = k_col + QK_WIDTH // dv
    return pl.pallas_call(
        functools.partial(_diff_attn_kernel, t=t, lambda_init=lambda_init),
        out_shape=jax.ShapeDtypeStruct((s, V_WIDTH), BF16),
        grid=(H_DIFF, s // t),
        in_specs=[
            pl.BlockSpec((s, dv), lambda h, i: (0, q_col + h)),
            pl.BlockSpec((s, dv), lambda h, i: (0, k_col + h)),
            pl.BlockSpec((s, dv), lambda h, i: (0, v_col + h)),
            pl.BlockSpec((None, 2, t, 2 * t), lambda h, i: (h, 0, 0, 0)),
            pl.BlockSpec((4, DH_DIFF), lambda h, i: (0, 0)),
            pl.BlockSpec((1, dv), lambda h, i: (0, 0)),
        ],
        out_specs=pl.BlockSpec((t, dv), lambda h, i: (i, h)),
        scratch_shapes=[
            pltpu.VMEM((2 * t, dv), BF16),
            pltpu.VMEM((dv, s), BF16),
            pltpu.VMEM((2 * t // Q_CHUNK, t, Q_CHUNK), F32),
            pltpu.VMEM((1, 2 * t), F32),
            pltpu.VMEM((1, 2 * t), F32),
            pltpu.VMEM((dv, 2 * t), F32),
        ],
        compiler_params=_params("arbitrary", "arbitrary"),
        name="diff_attn",
    )(proj, proj, proj, bias_tiles, lam_p, g)


def _mix_cross_kernel(a_ref, d_ref, x_ref, wo_ref, gc_ref, wcq_ref, ckv_ref, wco_ref, gf_ref,
                      xo_ref, hf_ref):
    x1 = x_ref[...] + _dot(a_ref[...], wo_ref[0:C_A, :]) + _dot(d_ref[...], wo_ref[C_A:, :])
    hc = _rms(x1, gc_ref[...], NORM_EPS).astype(BF16)
    cq = _dot(hc, wcq_ref[...]).astype(BF16)
    kv_off = H_CROSS * DH_CROSS
    heads = []
    for h in range(H_CROSS):
        lo, hi = h * DH_CROSS, (h + 1) * DH_CROSS
        cl = _dot_nt(cq[:, lo:hi], ckv_ref[:, lo:hi]) * (DH_CROSS ** -0.5)
        e = jnp.exp(cl - jnp.max(cl, axis=-1, keepdims=True))
        l = jnp.sum(e, axis=-1, keepdims=True)
        heads.append((_dot(e.astype(BF16), ckv_ref[:, kv_off + lo:kv_off + hi]) / l).astype(BF16))
    x2 = x1 + _dot(jnp.concatenate(heads, axis=1), wco_ref[...])
    xo_ref[...] = x2
    hf_ref[...] = _rms(x2, gf_ref[...], NORM_EPS).astype(hf_ref.dtype)


def _mix_cross(a, d, x, w_out, g_cross, w_cq, ckv, w_co, g_ffn, *, layer, tm):
    s, d_model = x.shape
    n_cq = w_cq.shape[2]
    const = lambda i: (0, 0)
    of_layer = lambda i: (layer, 0, 0)
    return pl.pallas_call(
        _mix_cross_kernel,
        out_shape=(jax.ShapeDtypeStruct((s, d_model), F32), jax.ShapeDtypeStruct((s, d_model), BF16)),
        grid=(s // tm,),
        in_specs=[
            pl.BlockSpec((tm, C_A), lambda i: (i, 0)),
            pl.BlockSpec((tm, V_WIDTH), lambda i: (i, 0)),
            pl.BlockSpec((tm, d_model), lambda i: (i, 0)),
            pl.BlockSpec((None, C_A + V_WIDTH, d_model), of_layer),
            pl.BlockSpec((1, d_model), const),
            pl.BlockSpec((None, d_model, n_cq), of_layer),
            pl.BlockSpec((None,) + ckv.shape[1:], of_layer),
            pl.BlockSpec((None, n_cq, d_model), of_layer),
            pl.BlockSpec((1, d_model), const),
        ],
        out_specs=(pl.BlockSpec((tm, d_model), lambda i: (i, 0)),
                   pl.BlockSpec((tm, d_model), lambda i: (i, 0))),
        compiler_params=_params("arbitrary"),
        name="mix_cross",
    )(a, d, x, w_out, g_cross, w_cq, ckv, w_co, g_ffn)


def _ffn_kernel(hf_ref, halo_ref, x_ref, wg_ref, wv_ref, cwg_ref, cwv_ref, cbg_ref, cbv_ref, wd_ref,
                o_ref, ext_ref, ug_ref, uv_ref, *, tm):
    @pl.when(pl.program_id(1) == 0)
    def _():
        halo = halo_ref[...]
        ext_ref[0:HALO_F, :] = jnp.where(pl.program_id(0) > 0, halo, jnp.zeros_like(halo))
        ext_ref[HALO_F:, :] = hf_ref[...]
        o_ref[...] = x_ref[...]

    ext = ext_ref[...]
    ug_ref[...] = _dot(ext, wg_ref[...])
    uv_ref[...] = _dot(ext, wv_ref[...])

    def conv(u_ref, cw_ref, cb_ref):
        out = cb_ref[...]
        for j in range(CONV_F_WIDTH):
            out = out + cw_ref[j:j + 1, :] * u_ref[pl.ds(HALO_F - (CONV_F_WIDTH - 1) + j, tm), :]
        return out

    y = jax.nn.silu(conv(ug_ref, cwg_ref, cbg_ref)) * conv(uv_ref, cwv_ref, cbv_ref)
    o_ref[...] += _dot(y.astype(BF16), wd_ref[...])


def _ffn(hf, x, w_up, cw, cb, w_down, *, layer, tm, tf):
    s, d_model = x.shape
    d_ff = w_down.shape[1]
    nf = d_ff // tf
    halo_blocks = tm // HALO_F
    return pl.pallas_call(
        functools.partial(_ffn_kernel, tm=tm),
        out_shape=jax.ShapeDtypeStruct((s, d_model), F32),
        grid=(s // tm, nf),
        in_specs=[
            pl.BlockSpec((tm, d_model), lambda i, f: (i, 0)),
            pl.BlockSpec((HALO_F, d_model), lambda i, f: (jnp.maximum(i * halo_blocks - 1, 0), 0)),
            pl.BlockSpec((tm, d_model), lambda i, f: (i, 0)),
            pl.BlockSpec((None, d_model, tf), lambda i, f: (layer, 0, f)),
            pl.BlockSpec((None, d_model, tf), lambda i, f: (layer, 0, nf + f)),
            pl.BlockSpec((CONV_F_WIDTH, tf), lambda i, f: (0, f)),
            pl.BlockSpec((CONV_F_WIDTH, tf), lambda i, f: (0, nf + f)),
            pl.BlockSpec((1, tf), lambda i, f: (0, f)),
            pl.BlockSpec((1, tf), lambda i, f: (0, nf + f)),
            pl.BlockSpec((None, tf, d_model), lambda i, f: (layer, f, 0)),
        ],
        out_specs=pl.BlockSpec((tm, d_model), lambda i, f: (i, 0)),
        scratch_shapes=[
            pltpu.VMEM((tm + HALO_F, d_model), BF16),
            pltpu.VMEM((tm + HALO_F, tf), F32),
            pltpu.VMEM((tm + HALO_F, tf), F32),
        ],
        compiler_params=_params("arbitrary", "arbitrary"),
        name="conv_ffn",
    )(hf, hf, x, w_up, w_up, cw, cw, cb, cb, w_down)


def _final_norm_kernel(x_ref, g_ref, o_ref):
    o_ref[...] = _rms(x_ref[...], g_ref[...], NORM_EPS)


def _final_norm(x, g, *, tm):
    s, d_model = x.shape
    return pl.pallas_call(
        _final_norm_kernel,
        out_shape=jax.ShapeDtypeStruct((s, d_model), F32),
        grid=(s // tm,),
        in_specs=[pl.BlockSpec((tm, d_model), lambda i: (i, 0)),
                  pl.BlockSpec((1, d_model), lambda i: (0, 0))],
        out_specs=pl.BlockSpec((tm, d_model), lambda i: (i, 0)),
        compiler_params=_params("arbitrary"),
        name="final_norm",
    )(x, g)


def kernel(x, mem, rel_bias_table, g_mix, w_in, conv_a_w, conv_a_b, ln_a_g, ln_a_b, diff_lambda, subln_g, w_out, g_cross, g_mem, w_cq, w_ckv, w_co, g_ffn, w_up, conv_f_w, conv_f_b, w_down, g_final):
    b, s, d_model = x.shape
    assert b == 1 and mem.shape[0] == 1
    depth = w_in.shape[0]
    d_ff = w_down.shape[1]

    t_attn = _pick_tile(s, 512)
    assert t_attn >= MAX_DISTANCE, "the two biased key blocks must cover every bucketed distance"
    tm_proj = _pick_tile(s, 512)
    tn_proj = _pick_tile(w_in.shape[2], 1024)
    ts_conv = _pick_tile(s, 256)
    tm_mix = _pick_tile(s, 256)
    tm_ffn = _pick_tile(s, 512)
    tf_ffn = _pick_tile(d_ff, 512)

    row = lambda p: p.reshape(depth, 1, p.shape[-1])
    w_in_b, w_out_b, w_cq_b = w_in.astype(BF16), w_out.astype(BF16), w_cq.astype(BF16)
    w_ckv_b, w_co_b = w_ckv.astype(BF16), w_co.astype(BF16)
    w_up_b, w_down_b = w_up.astype(BF16), w_down.astype(BF16)
    g_mix, g_cross, g_ffn, g_mem = row(g_mix), row(g_cross), row(g_ffn), row(g_mem)
    conv_a_b, ln_a_g, ln_a_b = row(conv_a_b), row(ln_a_g), row(ln_a_b)
    subln_g, conv_f_b = row(subln_g), row(conv_f_b)

    bias_tiles = _bias_tiles(rel_bias_table, t_attn)
    ckv = _mem_kv(mem[0], g_mem, w_ckv_b)

    xs = x[0]
    for l in range(depth):
        lambda_init = 0.8 - 0.6 * math.exp(-0.3 * l)
        proj = _in_proj(xs, g_mix[l], w_in_b, layer=l, tm=tm_proj, tn=tn_proj)
        a = _conv_module(proj, conv_a_w[l], conv_a_b[l], ln_a_g[l], ln_a_b[l], ts=ts_conv)
        d = _diff_attn(proj, bias_tiles, diff_lambda[l], subln_g[l], t=t_attn, lambda_init=lambda_init)
        xs, hf = _mix_cross(a, d, xs, w_out_b, g_cross[l], w_cq_b, ckv, w_co_b, g_ffn[l],
                            layer=l, tm=tm_mix)
        xs = _ffn(hf, xs, w_up_b, conv_f_w[l], conv_f_b[l], w_down_b, layer=l, tm=tm_ffn, tf=tf_ffn)
    return _final_norm(xs, g_final.reshape(1, d_model), tm=tm_proj)[None]
```

```python
import functools
import math

import jax
import jax.numpy as jnp
from jax import lax
from jax.experimental import pallas as pl
from jax.experimental.pallas import tpu as pltpu

N_MEM = 256
C_A = 1024
CONV_A_WIDTH = 31
H_DIFF = 8
DH_DIFF = 64
QK_WIDTH = 2 * H_DIFF * DH_DIFF
V_WIDTH = H_DIFF * 2 * DH_DIFF
N_BUCKETS = 32
MAX_DISTANCE = 128
H_CROSS = 4
DH_CROSS = 128
CONV_F_WIDTH = 3
NORM_EPS = 1e-6
SUBLN_EPS = 1e-5
NEG_INF = -1e30

LANES = 128
SUBLANES = 8
BF16_SUBLANES = 16
VMEM_LIMIT_BYTES = 56 * 1024 * 1024

F32 = jnp.float32
BF16 = jnp.bfloat16

HALO_A = 32
HALO_F = BF16_SUBLANES
VT_CHUNK = 256
Q_CHUNK = 256
FFN_ROW_PIECES = 4


def _params(*sem):
    return pltpu.CompilerParams(dimension_semantics=sem, vmem_limit_bytes=VMEM_LIMIT_BYTES)


def _rms(x, g, eps):
    return x * lax.rsqrt(jnp.mean(x * x, axis=-1, keepdims=True) + eps) * g


def _dot(a, b):
    return jnp.dot(a, b, preferred_element_type=F32)


def _dot_nt(a, b):
    return lax.dot_general(a, b, (((1,), (1,)), ((), ())), preferred_element_type=F32)


def _pick_tile(n, target):
    t = min(n, target)
    while n % t:
        t //= 2
    return t


def _bias_tiles_kernel(table_ref, o_ref, *, t):
    ki = lax.broadcasted_iota(jnp.int32, (t, t), 0)
    qi = lax.broadcasted_iota(jnp.int32, (t, t), 1)
    max_exact = N_BUCKETS // 2
    for j in range(2):
        n = qi - ki + j * t
        nf = jnp.maximum(n, 1).astype(F32)
        large = max_exact + (jnp.log(nf / max_exact) / math.log(MAX_DISTANCE / max_exact)
                             * (N_BUCKETS - max_exact)).astype(jnp.int32)
        large = jnp.minimum(large, N_BUCKETS - 1)
        bucket = jnp.where(n < max_exact, n, large)
        for h in range(H_DIFF):
            far = table_ref[N_BUCKETS - 1, h]
            b = jnp.zeros((t, t), F32)
            for k in range(N_BUCKETS - 1):
                b = jnp.where(bucket == k, table_ref[k, h] - far, b)
            if j == 0:
                b = jnp.where(n >= 0, b, NEG_INF)
            o_ref[h, j] = b


def _bias_tiles(table, t):
    return pl.pallas_call(
        functools.partial(_bias_tiles_kernel, t=t),
        out_shape=jax.ShapeDtypeStruct((H_DIFF, 2, t, t), F32),
        in_specs=[pl.BlockSpec(memory_space=pltpu.SMEM)],
        out_specs=pl.BlockSpec(memory_space=pltpu.VMEM),
        compiler_params=pltpu.CompilerParams(vmem_limit_bytes=VMEM_LIMIT_BYTES),
        name="bias_tiles",
    )(table)


def _mem_kv_kernel(mem_ref, g_ref, w_ref, o_ref):
    m = _rms(mem_ref[...], g_ref[...], NORM_EPS).astype(BF16)
    o_ref[...] = _dot(m, w_ref[...]).astype(o_ref.dtype)


def _mem_kv(mem, g_mem, w_ckv):
    depth, d_model, n_out = w_ckv.shape
    n_mem = mem.shape[0]
    return pl.pallas_call(
        _mem_kv_kernel,
        out_shape=jax.ShapeDtypeStruct((depth, n_mem, n_out), BF16),
        grid=(depth,),
        in_specs=[
            pl.BlockSpec((n_mem, d_model), lambda l: (0, 0)),
            pl.BlockSpec((None, 1, d_model), lambda l: (l, 0, 0)),
            pl.BlockSpec((None, d_model, n_out), lambda l: (l, 0, 0)),
        ],
        out_specs=pl.BlockSpec((None, n_mem, n_out), lambda l: (l, 0, 0)),
        compiler_params=_params("arbitrary"),
        name="mem_kv",
    )(mem, g_mem, w_ckv)


def _in_proj_kernel(x_ref, g_ref, w_ref, o_ref, h_ref):
    @pl.when(pl.program_id(1) == 0)
    def _():
        h_ref[...] = _rms(x_ref[...], g_ref[...], NORM_EPS).astype(h_ref.dtype)

    o_ref[...] = _dot(h_ref[...], w_ref[...]).astype(o_ref.dtype)


def _in_proj(x, g, w, *, layer, tm, tn):
    s, d_model = x.shape
    n = w.shape[2]
    return pl.pallas_call(
        _in_proj_kernel,
        out_shape=jax.ShapeDtypeStruct((s, n), BF16),
        grid=(s // tm, n // tn),
        in_specs=[
            pl.BlockSpec((tm, d_model), lambda i, j: (i, 0)),
            pl.BlockSpec((1, d_model), lambda i, j: (0, 0)),
            pl.BlockSpec((None, d_model, tn), lambda i, j: (layer, 0, j)),
        ],
        out_specs=pl.BlockSpec((tm, tn), lambda i, j: (i, j)),
        scratch_shapes=[pltpu.VMEM((tm, d_model), BF16)],
        compiler_params=_params("arbitrary", "arbitrary"),
        name="in_proj",
    )(x, g, w)


def _conv_module_kernel(main_ref, halo_ref, cw_ref, cb_ref, lg_ref, lb_ref, o_ref,
                        g_ref, sh_ref, conv_ref, *, ts, rows):
    def glu(v):
        v = v.astype(F32)
        return v[:, :C_A] * jax.nn.sigmoid(v[:, C_A:])

    g_ref[0:HALO_A, :] = jnp.where(pl.program_id(0) > 0, glu(halo_ref[...]), 0.0)
    g_ref[HALO_A:, :] = glu(main_ref[...])
    sh_rows = sh_ref.shape[1]
    for k in range(1, SUBLANES):
        sh_ref[k - 1] = g_ref[k:k + sh_rows, :]

    first_tap = HALO_A - (CONV_A_WIDTH - 1)
    groups = rows // SUBLANES

    width = C_A // 2
    for cols in (slice(0, width), slice(width, C_A)):
        def chunk(r, carry, cols=cols):
            base = pl.multiple_of(r * rows, rows)
            acc = jnp.broadcast_to(cb_ref[:, cols].reshape(1, 1, width), (groups, SUBLANES, width))
            for j in range(CONV_A_WIDTH):
                k = (first_tap + j) % SUBLANES
                src = g_ref if k == 0 else sh_ref.at[k - 1]
                win = src[pl.ds(base + first_tap + j - k, rows), cols]
                acc = acc + cw_ref[j, :, cols][None] * win.reshape(groups, SUBLANES, width)
            conv_ref[pl.ds(base, rows), cols] = acc.reshape(rows, width)
            return carry

        lax.fori_loop(0, ts // rows, chunk, 0)

    u = conv_ref[...]
    xc = u - jnp.mean(u, axis=-1, keepdims=True)
    y = xc * lax.rsqrt(jnp.mean(xc * xc, axis=-1, keepdims=True) + NORM_EPS)
    y = y * lg_ref[...] + lb_ref[...]
    o_ref[...] = (y * jax.nn.sigmoid(y)).astype(o_ref.dtype)


def _conv_module(proj, cw, cb, lg, lb, *, ts, rows=4 * SUBLANES):
    s = proj.shape[0]
    halo_blocks = ts // HALO_A
    cw = jnp.broadcast_to(cw[:, None, :], (CONV_A_WIDTH, SUBLANES, C_A))
    return pl.pallas_call(
        functools.partial(_conv_module_kernel, ts=ts, rows=rows),
        out_shape=jax.ShapeDtypeStruct((s, C_A), BF16),
        grid=(s // ts,),
        in_specs=[
            pl.BlockSpec((ts, 2 * C_A), lambda i: (i, 0)),
            pl.BlockSpec((HALO_A, 2 * C_A), lambda i: (jnp.maximum(i * halo_blocks - 1, 0), 0)),
            pl.BlockSpec((CONV_A_WIDTH, SUBLANES, C_A), lambda i: (0, 0, 0)),
            pl.BlockSpec((1, C_A), lambda i: (0, 0)),
            pl.BlockSpec((1, C_A), lambda i: (0, 0)),
            pl.BlockSpec((1, C_A), lambda i: (0, 0)),
        ],
        out_specs=pl.BlockSpec((ts, C_A), lambda i: (i, 0)),
        scratch_shapes=[pltpu.VMEM((ts + HALO_A, C_A), F32),
                        pltpu.VMEM((SUBLANES - 1, ts + HALO_A - SUBLANES, C_A), F32),
                        pltpu.VMEM((ts, C_A), F32)],
        compiler_params=_params("arbitrary"),
        name="conv_module",
    )(proj, proj, cw, cb, lg, lb)


def _diff_attn_kernel(q_ref, k_ref, v_ref, bias_ref, lam_ref, g_ref, o_ref,
                      qs_ref, vt_ref, s_ref, m_ref, l_ref, acc_ref, *, tq, tk, lambda_init):
    qi = pl.program_id(1)
    s_len = v_ref.shape[0]
    ratio = tq // tk
    chunks = [slice(c * Q_CHUNK, (c + 1) * Q_CHUNK) for c in range(2 * tq // Q_CHUNK)]
    sub_tile = [(c * Q_CHUNK % tq) // tk for c in range(len(chunks))]
    sub_cols = [slice(c * Q_CHUNK % tk, c * Q_CHUNK % tk + Q_CHUNK) for c in range(len(chunks))]

    def load_queries(tile):
        q = q_ref[pl.ds(pl.multiple_of(tile * tq, tq), tq), :] * (DH_DIFF ** -0.5)
        lane = lax.broadcasted_iota(jnp.int32, q.shape, 1)
        zero = jnp.zeros_like(q)
        qs_ref[0:tq, :] = jnp.where(lane < DH_DIFF, q, zero)
        qs_ref[tq:, :] = jnp.where(lane >= DH_DIFF, q, zero)

    def scores(kb, c):
        return _dot_nt(k_ref[pl.ds(pl.multiple_of(kb * tk, tk), tk), :], qs_ref[chunks[c], :])

    @pl.when(qi == 0)
    def _():
        for c in range(s_len // VT_CHUNK):
            vt_ref[:, c * VT_CHUNK:(c + 1) * VT_CHUNK] = v_ref[c * VT_CHUNK:(c + 1) * VT_CHUNK, :].T
        load_queries(0)
        for c in range(len(chunks)):
            s_ref[c] = scores(0, c)

    m_ref[...] = jnp.full(m_ref.shape, NEG_INF, F32)
    l_ref[...] = jnp.zeros(l_ref.shape, F32)
    acc_ref[...] = jnp.zeros(acc_ref.shape, F32)

    def step(kb, rel, next_kb, next_rel):
        vt_blk = vt_ref[:, pl.ds(pl.multiple_of(kb * tk, tk), tk)]
        for c, cols in enumerate(chunks):
            back = None if rel is None else sub_tile[c] - rel
            live = back is None or back >= 0
            if live:
                s = s_ref[c]
            if next_rel is None or sub_tile[c] - next_rel >= 0:
                s_ref[c] = scores(next_kb, c)
            if not live:
                continue
            if back is not None and back <= 1:
                s = s + bias_ref[back, :, sub_cols[c]]
            m_prev = m_ref[:, cols]
            m_new = jnp.maximum(m_prev, jnp.max(s, axis=0, keepdims=True))
            alpha = jnp.exp(m_prev - m_new)
            p = jnp.exp(s - m_new)
            l_ref[:, cols] = alpha * l_ref[:, cols] + jnp.sum(p, axis=0, keepdims=True)
            acc_ref[:, cols] = alpha * acc_ref[:, cols] + _dot(vt_blk, p.astype(BF16))
            m_ref[:, cols] = m_new

    first = ratio * qi

    def far_step(kb, carry):
        step(kb, None, kb + 1, None)
        return carry

    lax.fori_loop(0, jnp.maximum(first - 1, 0), far_step, 0)

    @pl.when(qi >= 1)
    def _():
        step(first - 1, -1, first, 0)

    for rel in range(ratio - 1):
        step(first + rel, rel, first + rel + 1, rel + 1)

    load_queries(jnp.minimum(qi + 1, pl.num_programs(1) - 1))
    step(first + ratio - 1, ratio - 1, 0, None)

    lam_p = lam_ref[...]
    lam = (jnp.exp(jnp.sum(lam_p[0:1] * lam_p[1:2], axis=-1, keepdims=True))
           - jnp.exp(jnp.sum(lam_p[2:3] * lam_p[3:4], axis=-1, keepdims=True)) + lambda_init)
    o = acc_ref[...] / l_ref[...]
    d = o[:, 0:tq] - lam * o[:, tq:]
    d = d * lax.rsqrt(jnp.mean(d * d, axis=0, keepdims=True) + SUBLN_EPS)
    o_ref[...] = (d.T * g_ref[...] * (1.0 - lambda_init)).astype(o_ref.dtype)


def _diff_attn(proj, bias_tiles, lam_p, g, *, tq, tk, lambda_init):
    s = proj.shape[0]
    dv = 2 * DH_DIFF
    q_col = 2 * C_A // dv
    k_col = q_col + QK_WIDTH // dv
    v_col = k_col + QK_WIDTH // dv
    return pl.pallas_call(
        functools.partial(_diff_attn_kernel, tq=tq, tk=tk, lambda_init=lambda_init),
        out_shape=jax.ShapeDtypeStruct((s, V_WIDTH), BF16),
        grid=(H_DIFF, s // tq),
        in_specs=[
            pl.BlockSpec((s, dv), lambda h, i: (0, q_col + h)),
            pl.BlockSpec((s, dv), lambda h, i: (0, k_col + h)),
            pl.BlockSpec((s, dv), lambda h, i: (0, v_col + h)),
            pl.BlockSpec((None, 2, tk, tk), lambda h, i: (h, 0, 0, 0)),
            pl.BlockSpec((4, DH_DIFF), lambda h, i: (0, 0)),
            pl.BlockSpec((1, dv), lambda h, i: (0, 0)),
        ],
        out_specs=pl.BlockSpec((tq, dv), lambda h, i: (i, h)),
        scratch_shapes=[
            pltpu.VMEM((2 * tq, dv), BF16),
            pltpu.VMEM((dv, s), BF16),
            pltpu.VMEM((2 * tq // Q_CHUNK, tk, Q_CHUNK), F32),
            pltpu.VMEM((1, 2 * tq), F32),
            pltpu.VMEM((1, 2 * tq), F32),
            pltpu.VMEM((dv, 2 * tq), F32),
        ],
        compiler_params=_params("arbitrary", "arbitrary"),
        name="diff_attn",
    )(proj, proj, proj, bias_tiles, lam_p, g)


def _mix_cross_kernel(a_ref, d_ref, x_ref, wo_ref, gc_ref, wcq_ref, ckv_ref, wco_ref, gf_ref,
                      xo_ref, hf_ref):
    x1 = x_ref[...] + _dot(a_ref[...], wo_ref[0:C_A, :]) + _dot(d_ref[...], wo_ref[C_A:, :])
    hc = _rms(x1, gc_ref[...], NORM_EPS).astype(BF16)
    cq = _dot(hc, wcq_ref[...]).astype(BF16)
    kv_off = H_CROSS * DH_CROSS
    heads = []
    for h in range(H_CROSS):
        lo, hi = h * DH_CROSS, (h + 1) * DH_CROSS
        cl = _dot_nt(cq[:, lo:hi], ckv_ref[:, lo:hi]) * (DH_CROSS ** -0.5)
        e = jnp.exp(cl - jnp.max(cl, axis=-1, keepdims=True))
        l = jnp.sum(e, axis=-1, keepdims=True)
        heads.append((_dot(e.astype(BF16), ckv_ref[:, kv_off + lo:kv_off + hi]) / l).astype(BF16))
    x2 = x1 + _dot(jnp.concatenate(heads, axis=1), wco_ref[...])
    xo_ref[...] = x2
    hf_ref[...] = _rms(x2, gf_ref[...], NORM_EPS).astype(hf_ref.dtype)


def _mix_cross(a, d, x, w_out, g_cross, w_cq, ckv, w_co, g_ffn, *, layer, tm):
    s, d_model = x.shape
    n_cq = w_cq.shape[2]
    const = lambda i: (0, 0)
    of_layer = lambda i: (layer, 0, 0)
    return pl.pallas_call(
        _mix_cross_kernel,
        out_shape=(jax.ShapeDtypeStruct((s, d_model), F32), jax.ShapeDtypeStruct((s, d_model), BF16)),
        grid=(s // tm,),
        in_specs=[
            pl.BlockSpec((tm, C_A), lambda i: (i, 0)),
            pl.BlockSpec((tm, V_WIDTH), lambda i: (i, 0)),
            pl.BlockSpec((tm, d_model), lambda i: (i, 0)),
            pl.BlockSpec((None, C_A + V_WIDTH, d_model), of_layer),
            pl.BlockSpec((1, d_model), const),
            pl.BlockSpec((None, d_model, n_cq), of_layer),
            pl.BlockSpec((None,) + ckv.shape[1:], of_layer),
            pl.BlockSpec((None, n_cq, d_model), of_layer),
            pl.BlockSpec((1, d_model), const),
        ],
        out_specs=(pl.BlockSpec((tm, d_model), lambda i: (i, 0)),
                   pl.BlockSpec((tm, d_model), lambda i: (i, 0))),
        compiler_params=_params("arbitrary"),
        name="mix_cross",
    )(a, d, x, w_out, g_cross, w_cq, ckv, w_co, g_ffn)


def _ffn_kernel(hf_ref, halo_ref, x_ref, wg_ref, wv_ref, cwg_ref, cwv_ref, cbg_ref, cbv_ref, wd_ref,
                o_ref, ext_ref, ug_ref, uv_ref, *, tm):
    @pl.when(pl.program_id(1) == 0)
    def _():
        halo = halo_ref[...]
        ext_ref[0:HALO_F, :] = jnp.where(pl.program_id(0) > 0, halo, jnp.zeros_like(halo))
        ext_ref[HALO_F:, :] = hf_ref[...]
        o_ref[...] = x_ref[...]

    ext = ext_ref[...]
    ug_ref[...] = _dot(ext, wg_ref[...])
    uv_ref[...] = _dot(ext, wv_ref[...])

    def conv(u_ref, cw_ref, cb_ref, row0, rows):
        out = cb_ref[...]
        for j in range(CONV_F_WIDTH):
            out = out + cw_ref[j:j + 1, :] * u_ref[pl.ds(HALO_F - (CONV_F_WIDTH - 1) + j + row0, rows), :]
        return out

    rows = tm // FFN_ROW_PIECES
    for r in range(FFN_ROW_PIECES):
        y = (jax.nn.silu(conv(ug_ref, cwg_ref, cbg_ref, r * rows, rows))
             * conv(uv_ref, cwv_ref, cbv_ref, r * rows, rows))
        o_ref[r * rows:(r + 1) * rows, :] += _dot(y.astype(BF16), wd_ref[...])


def _ffn(hf, x, w_up, cw, cb, w_down, *, layer, tm, tf):
    s, d_model = x.shape
    d_ff = w_down.shape[1]
    nf = d_ff // tf
    halo_blocks = tm // HALO_F
    return pl.pallas_call(
        functools.partial(_ffn_kernel, tm=tm),
        out_shape=jax.ShapeDtypeStruct((s, d_model), F32),
        grid=(s // tm, nf),
        in_specs=[
            pl.BlockSpec((tm, d_model), lambda i, f: (i, 0)),
            pl.BlockSpec((HALO_F, d_model), lambda i, f: (jnp.maximum(i * halo_blocks - 1, 0), 0)),
            pl.BlockSpec((tm, d_model), lambda i, f: (i, 0)),
            pl.BlockSpec((None, d_model, tf), lambda i, f: (layer, 0, f)),
            pl.BlockSpec((None, d_model, tf), lambda i, f: (layer, 0, nf + f)),
            pl.BlockSpec((CONV_F_WIDTH, tf), lambda i, f: (0, f)),
            pl.BlockSpec((CONV_F_WIDTH, tf), lambda i, f: (0, nf + f)),
            pl.BlockSpec((1, tf), lambda i, f: (0, f)),
            pl.BlockSpec((1, tf), lambda i, f: (0, nf + f)),
            pl.BlockSpec((None, tf, d_model), lambda i, f: (layer, f, 0)),
        ],
        out_specs=pl.BlockSpec((tm, d_model), lambda i, f: (i, 0)),
        scratch_shapes=[
            pltpu.VMEM((tm + HALO_F, d_model), BF16),
            pltpu.VMEM((tm + HALO_F, tf), F32),
            pltpu.VMEM((tm + HALO_F, tf), F32),
        ],
        compiler_params=_params("arbitrary", "arbitrary"),
        name="conv_ffn",
    )(hf, hf, x, w_up, w_up, cw, cw, cb, cb, w_down)


def _final_norm_kernel(x_ref, g_ref, o_ref):
    o_ref[...] = _rms(x_ref[...], g_ref[...], NORM_EPS)


def _final_norm(x, g, *, tm):
    s, d_model = x.shape
    return pl.pallas_call(
        _final_norm_kernel,
        out_shape=jax.ShapeDtypeStruct((s, d_model), F32),
        grid=(s // tm,),
        in_specs=[pl.BlockSpec((tm, d_model), lambda i: (i, 0)),
                  pl.BlockSpec((1, d_model), lambda i: (0, 0))],
        out_specs=pl.BlockSpec((tm, d_model), lambda i: (i, 0)),
        compiler_params=_params("arbitrary"),
        name="final_norm",
    )(x, g)


def kernel(x, mem, rel_bias_table, g_mix, w_in, conv_a_w, conv_a_b, ln_a_g, ln_a_b, diff_lambda, subln_g, w_out, g_cross, g_mem, w_cq, w_ckv, w_co, g_ffn, w_up, conv_f_w, conv_f_b, w_down, g_final):
    b, s, d_model = x.shape
    assert b == 1 and mem.shape[0] == 1
    depth = w_in.shape[0]
    d_ff = w_down.shape[1]

    tk_attn = _pick_tile(s, 512)
    tq_attn = _pick_tile(s, 1024)
    assert tk_attn >= MAX_DISTANCE, "the two biased key blocks must cover every bucketed distance"
    tm_proj = _pick_tile(s, 512)
    tn_proj = _pick_tile(w_in.shape[2], 1024)
    ts_conv = _pick_tile(s, 256)
    tm_mix = _pick_tile(s, 256)
    tm_ffn = _pick_tile(s, 512)
    tf_ffn = _pick_tile(d_ff, 512)

    row = lambda p: p.reshape(depth, 1, p.shape[-1])
    w_in_b, w_out_b, w_cq_b = w_in.astype(BF16), w_out.astype(BF16), w_cq.astype(BF16)
    w_ckv_b, w_co_b = w_ckv.astype(BF16), w_co.astype(BF16)
    w_up_b, w_down_b = w_up.astype(BF16), w_down.astype(BF16)
    g_mix, g_cross, g_ffn, g_mem = row(g_mix), row(g_cross), row(g_ffn), row(g_mem)
    conv_a_b, ln_a_g, ln_a_b = row(conv_a_b), row(ln_a_g), row(ln_a_b)
    subln_g, conv_f_b = row(subln_g), row(conv_f_b)

    bias_tiles = _bias_tiles(rel_bias_table, tk_attn)
    ckv = _mem_kv(mem[0], g_mem, w_ckv_b)

    xs = x[0]
    for l in range(depth):
        lambda_init = 0.8 - 0.6 * math.exp(-0.3 * l)
        proj = _in_proj(xs, g_mix[l], w_in_b, layer=l, tm=tm_proj, tn=tn_proj)
        a = _conv_module(proj, conv_a_w[l], conv_a_b[l], ln_a_g[l], ln_a_b[l], ts=ts_conv)
        d = _diff_attn(proj, bias_tiles, diff_lambda[l], subln_g[l], tq=tq_attn, tk=tk_attn,
                       lambda_init=lambda_init)
        xs, hf = _mix_cross(a, d, xs, w_out_b, g_cross[l], w_cq_b, ckv, w_co_b, g_ffn[l],
                            layer=l, tm=tm_mix)
        xs = _ffn(hf, xs, w_up_b, conv_f_w[l], conv_f_b[l], w_down_b, layer=l, tm=tm_ffn, tf=tf_ffn)
    return _final_norm(xs, g_final.reshape(1, d_model), tm=tm_proj)[None]
```

```python
import functools
import math

import jax
import jax.numpy as jnp
from jax import lax
from jax.experimental import pallas as pl
from jax.experimental.pallas import tpu as pltpu

N_MEM = 256
C_A = 1024
CONV_A_WIDTH = 31
H_DIFF = 8
DH_DIFF = 64
QK_WIDTH = 2 * H_DIFF * DH_DIFF
V_WIDTH = H_DIFF * 2 * DH_DIFF
N_BUCKETS = 32
MAX_DISTANCE = 128
H_CROSS = 4
DH_CROSS = 128
CONV_F_WIDTH = 3
NORM_EPS = 1e-6
SUBLN_EPS = 1e-5
NEG_INF = -1e30
LOG2E = math.log2(math.e)

LANES = 128
SUBLANES = 8
BF16_SUBLANES = 16
VMEM_LIMIT_BYTES = 56 * 1024 * 1024

F32 = jnp.float32
BF16 = jnp.bfloat16

HALO_A = 32
HALO_F = BF16_SUBLANES
VT_CHUNK = 256
Q_CHUNK = 256
FFN_ROW_PIECES = 4


def _params(*sem):
    return pltpu.CompilerParams(dimension_semantics=sem, vmem_limit_bytes=VMEM_LIMIT_BYTES)


def _rms(x, g, eps):
    return x * lax.rsqrt(jnp.mean(x * x, axis=-1, keepdims=True) + eps) * g


def _dot(a, b):
    return jnp.dot(a, b, preferred_element_type=F32)


def _dot_nt(a, b):
    return lax.dot_general(a, b, (((1,), (1,)), ((), ())), preferred_element_type=F32)


def _pick_tile(n, target):
    t = min(n, target)
    while n % t:
        t //= 2
    return t


def _bias_tiles_kernel(table_ref, o_ref, *, t):
    ki = lax.broadcasted_iota(jnp.int32, (t, t), 0)
    qi = lax.broadcasted_iota(jnp.int32, (t, t), 1)
    max_exact = N_BUCKETS // 2
    for j in range(2):
        n = qi - ki + j * t
        nf = jnp.maximum(n, 1).astype(F32)
        large = max_exact + (jnp.log(nf / max_exact) / math.log(MAX_DISTANCE / max_exact)
                             * (N_BUCKETS - max_exact)).astype(jnp.int32)
        large = jnp.minimum(large, N_BUCKETS - 1)
        bucket = jnp.where(n < max_exact, n, large)
        for h in range(H_DIFF):
            far = table_ref[N_BUCKETS - 1, h]
            b = jnp.zeros((t, t), F32)
            for k in range(N_BUCKETS - 1):
                b = jnp.where(bucket == k, (table_ref[k, h] - far) * LOG2E, b)
            if j == 0:
                b = jnp.where(n >= 0, b, NEG_INF)
            o_ref[h, j] = b


def _bias_tiles(table, t):
    return pl.pallas_call(
        functools.partial(_bias_tiles_kernel, t=t),
        out_shape=jax.ShapeDtypeStruct((H_DIFF, 2, t, t), F32),
        in_specs=[pl.BlockSpec(memory_space=pltpu.SMEM)],
        out_specs=pl.BlockSpec(memory_space=pltpu.VMEM),
        compiler_params=pltpu.CompilerParams(vmem_limit_bytes=VMEM_LIMIT_BYTES),
        name="bias_tiles",
    )(table)


def _mem_kv_kernel(mem_ref, g_ref, w_ref, o_ref):
    m = _rms(mem_ref[...], g_ref[...], NORM_EPS).astype(BF16)
    o_ref[...] = _dot(m, w_ref[...]).astype(o_ref.dtype)


def _mem_kv(mem, g_mem, w_ckv):
    depth, d_model, n_out = w_ckv.shape
    n_mem = mem.shape[0]
    return pl.pallas_call(
        _mem_kv_kernel,
        out_shape=jax.ShapeDtypeStruct((depth, n_mem, n_out), BF16),
        grid=(depth,),
        in_specs=[
            pl.BlockSpec((n_mem, d_model), lambda l: (0, 0)),
            pl.BlockSpec((None, 1, d_model), lambda l: (l, 0, 0)),
            pl.BlockSpec((None, d_model, n_out), lambda l: (l, 0, 0)),
        ],
        out_specs=pl.BlockSpec((None, n_mem, n_out), lambda l: (l, 0, 0)),
        compiler_params=_params("arbitrary"),
        name="mem_kv",
    )(mem, g_mem, w_ckv)


def _in_proj_kernel(x_ref, g_ref, w_ref, scale_ref, o_ref, h_ref):
    @pl.when(pl.program_id(1) == 0)
    def _():
        h_ref[...] = _rms(x_ref[...], g_ref[...], NORM_EPS).astype(h_ref.dtype)

    o_ref[...] = (_dot(h_ref[...], w_ref[...]) * scale_ref[...]).astype(o_ref.dtype)


def _in_proj(x, g, w, *, layer, tm, tn):
    s, d_model = x.shape
    n = w.shape[2]
    col = jnp.arange(n)
    is_q = jnp.logical_and(col >= 2 * C_A, col < 2 * C_A + QK_WIDTH)
    scale = jnp.where(is_q, DH_DIFF ** -0.5 * LOG2E, 1.0).astype(F32).reshape(1, n)
    return pl.pallas_call(
        _in_proj_kernel,
        out_shape=jax.ShapeDtypeStruct((s, n), BF16),
        grid=(s // tm, n // tn),
        in_specs=[
            pl.BlockSpec((tm, d_model), lambda i, j: (i, 0)),
            pl.BlockSpec((1, d_model), lambda i, j: (0, 0)),
            pl.BlockSpec((None, d_model, tn), lambda i, j: (layer, 0, j)),
            pl.BlockSpec((1, tn), lambda i, j: (0, j)),
        ],
        out_specs=pl.BlockSpec((tm, tn), lambda i, j: (i, j)),
        scratch_shapes=[pltpu.VMEM((tm, d_model), BF16)],
        compiler_params=_params("arbitrary", "arbitrary"),
        name="in_proj",
    )(x, g, w, scale)


def _conv_module_kernel(main_ref, halo_ref, cw_ref, cb_ref, lg_ref, lb_ref, o_ref,
                        g_ref, sh_ref, conv_ref, *, ts, rows):
    def glu(v):
        v = v.astype(F32)
        return v[:, :C_A] * jax.nn.sigmoid(v[:, C_A:])

    g_ref[0:HALO_A, :] = jnp.where(pl.program_id(0) > 0, glu(halo_ref[...]), 0.0)
    g_ref[HALO_A:, :] = glu(main_ref[...])
    sh_rows = sh_ref.shape[1]
    for k in range(1, SUBLANES):
        sh_ref[k - 1] = g_ref[k:k + sh_rows, :]

    first_tap = HALO_A - (CONV_A_WIDTH - 1)
    groups = rows // SUBLANES

    width = C_A // 2
    for cols in (slice(0, width), slice(width, C_A)):
        def chunk(r, carry, cols=cols):
            base = pl.multiple_of(r * rows, rows)
            acc = jnp.broadcast_to(cb_ref[:, cols].reshape(1, 1, width), (groups, SUBLANES, width))
            for j in range(CONV_A_WIDTH):
                k = (first_tap + j) % SUBLANES
                src = g_ref if k == 0 else sh_ref.at[k - 1]
                win = src[pl.ds(base + first_tap + j - k, rows), cols]
                acc = acc + cw_ref[j, :, cols][None] * win.reshape(groups, SUBLANES, width)
            conv_ref[pl.ds(base, rows), cols] = acc.reshape(rows, width)
            return carry

        lax.fori_loop(0, ts // rows, chunk, 0)

    u = conv_ref[...]
    xc = u - jnp.mean(u, axis=-1, keepdims=True)
    y = xc * lax.rsqrt(jnp.mean(xc * xc, axis=-1, keepdims=True) + NORM_EPS)
    y = y * lg_ref[...] + lb_ref[...]
    o_ref[...] = (y * jax.nn.sigmoid(y)).astype(o_ref.dtype)


def _conv_module(proj, cw, cb, lg, lb, *, ts, rows=4 * SUBLANES):
    s = proj.shape[0]
    halo_blocks = ts // HALO_A
    cw = jnp.broadcast_to(cw[:, None, :], (CONV_A_WIDTH, SUBLANES, C_A))
    return pl.pallas_call(
        functools.partial(_conv_module_kernel, ts=ts, rows=rows),
        out_shape=jax.ShapeDtypeStruct((s, C_A), BF16),
        grid=(s // ts,),
        in_specs=[
            pl.BlockSpec((ts, 2 * C_A), lambda i: (i, 0)),
            pl.BlockSpec((HALO_A, 2 * C_A), lambda i: (jnp.maximum(i * halo_blocks - 1, 0), 0)),
            pl.BlockSpec((CONV_A_WIDTH, SUBLANES, C_A), lambda i: (0, 0, 0)),
            pl.BlockSpec((1, C_A), lambda i: (0, 0)),
            pl.BlockSpec((1, C_A), lambda i: (0, 0)),
            pl.BlockSpec((1, C_A), lambda i: (0, 0)),
        ],
        out_specs=pl.BlockSpec((ts, C_A), lambda i: (i, 0)),
        scratch_shapes=[pltpu.VMEM((ts + HALO_A, C_A), F32),
                        pltpu.VMEM((SUBLANES - 1, ts + HALO_A - SUBLANES, C_A), F32),
                        pltpu.VMEM((ts, C_A), F32)],
        compiler_params=_params("arbitrary"),
        name="conv_module",
    )(proj, proj, cw, cb, lg, lb)


def _diff_attn_kernel(q_ref, k_ref, v_ref, bias_ref, lam_ref, g_ref, o_ref,
                      qs_ref, vt_ref, s_ref, m_ref, l_ref, acc_ref, *, tq, tk, lambda_init):
    qi = pl.program_id(1)
    s_len = v_ref.shape[0]
    ratio = tq // tk
    chunks = [slice(c * Q_CHUNK, (c + 1) * Q_CHUNK) for c in range(2 * tq // Q_CHUNK)]
    sub_tile = [(c * Q_CHUNK % tq) // tk for c in range(len(chunks))]
    sub_cols = [slice(c * Q_CHUNK % tk, c * Q_CHUNK % tk + Q_CHUNK) for c in range(len(chunks))]

    def load_queries(tile):
        q = q_ref[pl.ds(pl.multiple_of(tile * tq, tq), tq), :]
        lane = lax.broadcasted_iota(jnp.int32, q.shape, 1)
        zero = jnp.zeros_like(q)
        qs_ref[0:tq, :] = jnp.where(lane < DH_DIFF, q, zero)
        qs_ref[tq:, :] = jnp.where(lane >= DH_DIFF, q, zero)

    def scores(kb, c):
        return _dot_nt(k_ref[pl.ds(pl.multiple_of(kb * tk, tk), tk), :], qs_ref[chunks[c], :])

    @pl.when(qi == 0)
    def _():
        for c in range(s_len // VT_CHUNK):
            vt_ref[:, c * VT_CHUNK:(c + 1) * VT_CHUNK] = v_ref[c * VT_CHUNK:(c + 1) * VT_CHUNK, :].T
        load_queries(0)
        for c in range(len(chunks)):
            s_ref[c] = scores(0, c)

    m_ref[...] = jnp.full(m_ref.shape, NEG_INF, F32)
    l_ref[...] = jnp.zeros(l_ref.shape, F32)
    acc_ref[...] = jnp.zeros(acc_ref.shape, F32)

    def step(kb, rel, next_kb, next_rel):
        vt_blk = vt_ref[:, pl.ds(pl.multiple_of(kb * tk, tk), tk)]
        for c, cols in enumerate(chunks):
            back = None if rel is None else sub_tile[c] - rel
            live = back is None or back >= 0
            if live:
                s = s_ref[c]
            if next_rel is None or sub_tile[c] - next_rel >= 0:
                s_ref[c] = scores(next_kb, c)
            if not live:
                continue
            if back is not None and back <= 1:
                s = s + bias_ref[back, :, sub_cols[c]]
            m_prev = m_ref[:, cols]
            m_new = jnp.maximum(m_prev, jnp.max(s, axis=0, keepdims=True))
            alpha = jnp.exp2(m_prev - m_new)
            p = jnp.exp2(s - m_new)
            l_ref[:, cols] = alpha * l_ref[:, cols] + jnp.sum(p, axis=0, keepdims=True)
            acc_ref[:, cols] = alpha * acc_ref[:, cols] + _dot(vt_blk, p.astype(BF16))
            m_ref[:, cols] = m_new

    first = ratio * qi

    def far_step(kb, carry):
        step(kb, None, kb + 1, None)
        return carry

    lax.fori_loop(0, jnp.maximum(first - 1, 0), far_step, 0)

    @pl.when(qi >= 1)
    def _():
        step(first - 1, -1, first, 0)

    for rel in range(ratio - 1):
        step(first + rel, rel, first + rel + 1, rel + 1)

    load_queries(jnp.minimum(qi + 1, pl.num_programs(1) - 1))
    step(first + ratio - 1, ratio - 1, 0, None)

    lam_p = lam_ref[...]
    lam = (jnp.exp(jnp.sum(lam_p[0:1] * lam_p[1:2], axis=-1, keepdims=True))
           - jnp.exp(jnp.sum(lam_p[2:3] * lam_p[3:4], axis=-1, keepdims=True)) + lambda_init)
    o = acc_ref[...] / l_ref[...]
    d = o[:, 0:tq] - lam * o[:, tq:]
    d = d * lax.rsqrt(jnp.mean(d * d, axis=0, keepdims=True) + SUBLN_EPS)
    o_ref[...] = (d.T * g_ref[...] * (1.0 - lambda_init)).astype(o_ref.dtype)


def _diff_attn(proj, bias_tiles, lam_p, g, *, tq, tk, lambda_init):
    s = proj.shape[0]
    dv = 2 * DH_DIFF
    q_col = 2 * C_A // dv
    k_col = q_col + QK_WIDTH // dv
    v_col = k_col + QK_WIDTH // dv
    return pl.pallas_call(
        functools.partial(_diff_attn_kernel, tq=tq, tk=tk, lambda_init=lambda_init),
        out_shape=jax.ShapeDtypeStruct((s, V_WIDTH), BF16),
        grid=(H_DIFF, s // tq),
        in_specs=[
            pl.BlockSpec((s, dv), lambda h, i: (0, q_col + h)),
            pl.BlockSpec((s, dv), lambda h, i: (0, k_col + h)),
            pl.BlockSpec((s, dv), lambda h, i: (0, v_col + h)),
            pl.BlockSpec((None, 2, tk, tk), lambda h, i: (h, 0, 0, 0)),
            pl.BlockSpec((4, DH_DIFF), lambda h, i: (0, 0)),
            pl.BlockSpec((1, dv), lambda h, i: (0, 0)),
        ],
        out_specs=pl.BlockSpec((tq, dv), lambda h, i: (i, h)),
        scratch_shapes=[
            pltpu.VMEM((2 * tq, dv), BF16),
            pltpu.VMEM((dv, s), BF16),
            pltpu.VMEM((2 * tq // Q_CHUNK, tk, Q_CHUNK), F32),
            pltpu.VMEM((1, 2 * tq), F32),
            pltpu.VMEM((1, 2 * tq), F32),
            pltpu.VMEM((dv, 2 * tq), F32),
        ],
        compiler_params=_params("arbitrary", "arbitrary"),
        name="diff_attn",
    )(proj, proj, proj, bias_tiles, lam_p, g)


def _mix_cross_kernel(a_ref, d_ref, x_ref, wo_ref, gc_ref, wcq_ref, ckv_ref, wco_ref, gf_ref,
                      xo_ref, hf_ref):
    x1 = x_ref[...] + _dot(a_ref[...], wo_ref[0:C_A, :]) + _dot(d_ref[...], wo_ref[C_A:, :])
    hc = _rms(x1, gc_ref[...], NORM_EPS).astype(BF16)
    cq = _dot(hc, wcq_ref[...]).astype(BF16)
    kv_off = H_CROSS * DH_CROSS
    heads = []
    for h in range(H_CROSS):
        lo, hi = h * DH_CROSS, (h + 1) * DH_CROSS
        cl = _dot_nt(cq[:, lo:hi], ckv_ref[:, lo:hi]) * (DH_CROSS ** -0.5)
        e = jnp.exp(cl - jnp.max(cl, axis=-1, keepdims=True))
        l = jnp.sum(e, axis=-1, keepdims=True)
        heads.append((_dot(e.astype(BF16), ckv_ref[:, kv_off + lo:kv_off + hi]) / l).astype(BF16))
    x2 = x1 + _dot(jnp.concatenate(heads, axis=1), wco_ref[...])
    xo_ref[...] = x2
    hf_ref[...] = _rms(x2, gf_ref[...], NORM_EPS).astype(hf_ref.dtype)


def _mix_cross(a, d, x, w_out, g_cross, w_cq, ckv, w_co, g_ffn, *, layer, tm):
    s, d_model = x.shape
    n_cq = w_cq.shape[2]
    const = lambda i: (0, 0)
    of_layer = lambda i: (layer, 0, 0)
    return pl.pallas_call(
        _mix_cross_kernel,
        out_shape=(jax.ShapeDtypeStruct((s, d_model), F32), jax.ShapeDtypeStruct((s, d_model), BF16)),
        grid=(s // tm,),
        in_specs=[
            pl.BlockSpec((tm, C_A), lambda i: (i, 0)),
            pl.BlockSpec((tm, V_WIDTH), lambda i: (i, 0)),
            pl.BlockSpec((tm, d_model), lambda i: (i, 0)),
            pl.BlockSpec((None, C_A + V_WIDTH, d_model), of_layer),
            pl.BlockSpec((1, d_model), const),
            pl.BlockSpec((None, d_model, n_cq), of_layer),
            pl.BlockSpec((None,) + ckv.shape[1:], of_layer),
            pl.BlockSpec((None, n_cq, d_model), of_layer),
            pl.BlockSpec((1, d_model), const),
        ],
        out_specs=(pl.BlockSpec((tm, d_model), lambda i: (i, 0)),
                   pl.BlockSpec((tm, d_model), lambda i: (i, 0))),
        compiler_params=_params("arbitrary"),
        name="mix_cross",
    )(a, d, x, w_out, g_cross, w_cq, ckv, w_co, g_ffn)


def _ffn_kernel(hf_ref, halo_ref, x_ref, wg_ref, wv_ref, cwg_ref, cwv_ref, cbg_ref, cbv_ref, wd_ref,
                o_ref, ext_ref, ug_ref, uv_ref, *, tm):
    @pl.when(pl.program_id(1) == 0)
    def _():
        halo = halo_ref[...]
        ext_ref[0:HALO_F, :] = jnp.where(pl.program_id(0) > 0, halo, jnp.zeros_like(halo))
        ext_ref[HALO_F:, :] = hf_ref[...]
        o_ref[...] = x_ref[...]

    ext = ext_ref[...]
    ug_ref[...] = _dot(ext, wg_ref[...])
    uv_ref[...] = _dot(ext, wv_ref[...])

    def conv(u_ref, cw_ref, cb_ref, row0, rows):
        out = cb_ref[...]
        for j in range(CONV_F_WIDTH):
            out = out + cw_ref[j:j + 1, :] * u_ref[pl.ds(HALO_F - (CONV_F_WIDTH - 1) + j + row0, rows), :]
        return out

    rows = tm // FFN_ROW_PIECES
    for r in range(FFN_ROW_PIECES):
        y = (jax.nn.silu(conv(ug_ref, cwg_ref, cbg_ref, r * rows, rows))
             * conv(uv_ref, cwv_ref, cbv_ref, r * rows, rows))
        o_ref[r * rows:(r + 1) * rows, :] += _dot(y.astype(BF16), wd_ref[...])


def _ffn(hf, x, w_up, cw, cb, w_down, *, layer, tm, tf):
    s, d_model = x.shape
    d_ff = w_down.shape[1]
    nf = d_ff // tf
    halo_blocks = tm // HALO_F
    return pl.pallas_call(
        functools.partial(_ffn_kernel, tm=tm),
        out_shape=jax.ShapeDtypeStruct((s, d_model), F32),
        grid=(s // tm, nf),
        in_specs=[
            pl.BlockSpec((tm, d_model), lambda i, f: (i, 0)),
            pl.BlockSpec((HALO_F, d_model), lambda i, f: (jnp.maximum(i * halo_blocks - 1, 0), 0)),
            pl.BlockSpec((tm, d_model), lambda i, f: (i, 0)),
            pl.BlockSpec((None, d_model, tf), lambda i, f: (layer, 0, f)),
            pl.BlockSpec((None, d_model, tf), lambda i, f: (layer, 0, nf + f)),
            pl.BlockSpec((CONV_F_WIDTH, tf), lambda i, f: (0, f)),
            pl.BlockSpec((CONV_F_WIDTH, tf), lambda i, f: (0, nf + f)),
            pl.BlockSpec((1, tf), lambda i, f: (0, f)),
            pl.BlockSpec((1, tf), lambda i, f: (0, nf + f)),
            pl.BlockSpec((None, tf, d_model), lambda i, f: (layer, f, 0)),
        ],
        out_specs=pl.BlockSpec((tm, d_model), lambda i, f: (i, 0)),
        scratch_shapes=[
            pltpu.VMEM((tm + HALO_F, d_model), BF16),
            pltpu.VMEM((tm + HALO_F, tf), F32),
            pltpu.VMEM((tm + HALO_F, tf), F32),
        ],
        compiler_params=_params("arbitrary", "arbitrary"),
        name="conv_ffn",
    )(hf, hf, x, w_up, w_up, cw, cw, cb, cb, w_down)


def _final_norm_kernel(x_ref, g_ref, o_ref):
    o_ref[...] = _rms(x_ref[...], g_ref[...], NORM_EPS)


def _final_norm(x, g, *, tm):
    s, d_model = x.shape
    return pl.pallas_call(
        _final_norm_kernel,
        out_shape=jax.ShapeDtypeStruct((s, d_model), F32),
        grid=(s // tm,),
        in_specs=[pl.BlockSpec((tm, d_model), lambda i: (i, 0)),
                  pl.BlockSpec((1, d_model), lambda i: (0, 0))],
        out_specs=pl.BlockSpec((tm, d_model), lambda i: (i, 0)),
        compiler_params=_params("arbitrary"),
        name="final_norm",
    )(x, g)


def kernel(x, mem, rel_bias_table, g_mix, w_in, conv_a_w, conv_a_b, ln_a_g, ln_a_b, diff_lambda, subln_g, w_out, g_cross, g_mem, w_cq, w_ckv, w_co, g_ffn, w_up, conv_f_w, conv_f_b, w_down, g_final):
    b, s, d_model = x.shape
    assert b == 1 and mem.shape[0] == 1
    depth = w_in.shape[0]
    d_ff = w_down.shape[1]

    tk_attn = _pick_tile(s, 512)
    tq_attn = _pick_tile(s, 1024)
    assert tk_attn >= MAX_DISTANCE, "the two biased key blocks must cover every bucketed distance"
    tm_proj = _pick_tile(s, 512)
    tn_proj = _pick_tile(w_in.shape[2], 1024)
    ts_conv = _pick_tile(s, 256)
    tm_mix = _pick_tile(s, 256)
    tm_ffn = _pick_tile(s, 512)
    tf_ffn = _pick_tile(d_ff, 512)

    row = lambda p: p.reshape(depth, 1, p.shape[-1])
    w_in_b, w_out_b, w_cq_b = w_in.astype(BF16), w_out.astype(BF16), w_cq.astype(BF16)
    w_ckv_b, w_co_b = w_ckv.astype(BF16), w_co.astype(BF16)
    w_up_b, w_down_b = w_up.astype(BF16), w_down.astype(BF16)
    g_mix, g_cross, g_ffn, g_mem = row(g_mix), row(g_cross), row(g_ffn), row(g_mem)
    conv_a_b, ln_a_g, ln_a_b = row(conv_a_b), row(ln_a_g), row(ln_a_b)
    subln_g, conv_f_b = row(subln_g), row(conv_f_b)

    bias_tiles = _bias_tiles(rel_bias_table, tk_attn)
    ckv = _mem_kv(mem[0], g_mem, w_ckv_b)

    xs = x[0]
    for l in range(depth):
        lambda_init = 0.8 - 0.6 * math.exp(-0.3 * l)
        proj = _in_proj(xs, g_mix[l], w_in_b, layer=l, tm=tm_proj, tn=tn_proj)
        a = _conv_module(proj, conv_a_w[l], conv_a_b[l], ln_a_g[l], ln_a_b[l], ts=ts_conv)
        d = _diff_attn(proj, bias_tiles, diff_lambda[l], subln_g[l], tq=tq_attn, tk=tk_attn,
                       lambda_init=lambda_init)
        xs, hf = _mix_cross(a, d, xs, w_out_b, g_cross[l], w_cq_b, ckv, w_co_b, g_ffn[l],
                            layer=l, tm=tm_mix)
        xs = _ffn(hf, xs, w_up_b, conv_f_w[l], conv_f_b[l], w_down_b, layer=l, tm=tm_ffn, tf=tf_ffn)
    return _final_norm(xs, g_final.reshape(1, d_model), tm=tm_proj)[None]
```

```python
import functools
import math

import jax
import jax.numpy as jnp
from jax import lax
from jax.experimental import pallas as pl
from jax.experimental.pallas import tpu as pltpu

N_MEM = 256
C_A = 1024
CONV_A_WIDTH = 31
H_DIFF = 8
DH_DIFF = 64
QK_WIDTH = 2 * H_DIFF * DH_DIFF
V_WIDTH = H_DIFF * 2 * DH_DIFF
N_BUCKETS = 32
MAX_DISTANCE = 128
H_CROSS = 4
DH_CROSS = 128
CONV_F_WIDTH = 3
NORM_EPS = 1e-6
SUBLN_EPS = 1e-5
NEG_INF = -1e30
LOG2E = math.log2(math.e)

LANES = 128
SUBLANES = 8
BF16_SUBLANES = 16
VMEM_LIMIT_BYTES = 56 * 1024 * 1024

F32 = jnp.float32
BF16 = jnp.bfloat16

HALO_A = 32
HALO_F = BF16_SUBLANES
VT_CHUNK = 256
Q_CHUNK = 256
FFN_ROW_PIECES = 4
PREFETCH_AHEAD = 2


def _params(*sem):
    return pltpu.CompilerParams(dimension_semantics=sem, vmem_limit_bytes=VMEM_LIMIT_BYTES)


def _rms(x, g, eps):
    return x * lax.rsqrt(jnp.mean(x * x, axis=-1, keepdims=True) + eps) * g


def _dot(a, b):
    return jnp.dot(a, b, preferred_element_type=F32)


def _dot_nt(a, b):
    return lax.dot_general(a, b, (((1,), (1,)), ((), ())), preferred_element_type=F32)


def _pick_tile(n, target):
    t = min(n, target)
    while n % t:
        t //= 2
    return t


def _bias_tiles_kernel(table_ref, o_ref, *, t):
    ki = lax.broadcasted_iota(jnp.int32, (t, t), 0)
    qi = lax.broadcasted_iota(jnp.int32, (t, t), 1)
    max_exact = N_BUCKETS // 2
    for j in range(2):
        n = qi - ki + j * t
        nf = jnp.maximum(n, 1).astype(F32)
        large = max_exact + (jnp.log(nf / max_exact) / math.log(MAX_DISTANCE / max_exact)
                             * (N_BUCKETS - max_exact)).astype(jnp.int32)
        large = jnp.minimum(large, N_BUCKETS - 1)
        bucket = jnp.where(n < max_exact, n, large)
        for h in range(H_DIFF):
            far = table_ref[N_BUCKETS - 1, h]
            b = jnp.zeros((t, t), F32)
            for k in range(N_BUCKETS - 1):
                b = jnp.where(bucket == k, (table_ref[k, h] - far) * LOG2E, b)
            if j == 0:
                b = jnp.where(n >= 0, b, NEG_INF)
            o_ref[h, j] = b


def _bias_tiles(table, t):
    return pl.pallas_call(
        functools.partial(_bias_tiles_kernel, t=t),
        out_shape=jax.ShapeDtypeStruct((H_DIFF, 2, t, t), F32),
        in_specs=[pl.BlockSpec(memory_space=pltpu.SMEM)],
        out_specs=pl.BlockSpec(memory_space=pltpu.VMEM),
        compiler_params=pltpu.CompilerParams(vmem_limit_bytes=VMEM_LIMIT_BYTES),
        name="bias_tiles",
    )(table)


def _mem_kv_kernel(mem_ref, g_ref, w_ref, o_ref):
    m = _rms(mem_ref[...], g_ref[...], NORM_EPS).astype(BF16)
    o_ref[...] = _dot(m, w_ref[...]).astype(o_ref.dtype)


def _mem_kv(mem, g_mem, w_ckv):
    depth, d_model, n_out = w_ckv.shape
    n_mem = mem.shape[0]
    return pl.pallas_call(
        _mem_kv_kernel,
        out_shape=jax.ShapeDtypeStruct((depth, n_mem, n_out), BF16),
        grid=(depth,),
        in_specs=[
            pl.BlockSpec((n_mem, d_model), lambda l: (0, 0)),
            pl.BlockSpec((None, 1, d_model), lambda l: (l, 0, 0)),
            pl.BlockSpec((None, d_model, n_out), lambda l: (l, 0, 0)),
        ],
        out_specs=pl.BlockSpec((None, n_mem, n_out), lambda l: (l, 0, 0)),
        compiler_params=_params("arbitrary"),
        name="mem_kv",
    )(mem, g_mem, w_ckv)


def _in_proj_kernel(x_ref, g_ref, w_ref, scale_ref, o_ref, h_ref):
    @pl.when(pl.program_id(1) == 0)
    def _():
        h_ref[...] = _rms(x_ref[...], g_ref[...], NORM_EPS).astype(h_ref.dtype)

    o_ref[...] = (_dot(h_ref[...], w_ref[...]) * scale_ref[...]).astype(o_ref.dtype)


def _in_proj(x, g, w, *, layer, tm, tn):
    s, d_model = x.shape
    n = w.shape[2]
    col = jnp.arange(n)
    is_q = jnp.logical_and(col >= 2 * C_A, col < 2 * C_A + QK_WIDTH)
    scale = jnp.where(is_q, DH_DIFF ** -0.5 * LOG2E, 1.0).astype(F32).reshape(1, n)
    return pl.pallas_call(
        _in_proj_kernel,
        out_shape=jax.ShapeDtypeStruct((s, n), BF16),
        grid=(s // tm, n // tn),
        in_specs=[
            pl.BlockSpec((tm, d_model), lambda i, j: (i, 0)),
            pl.BlockSpec((1, d_model), lambda i, j: (0, 0)),
            pl.BlockSpec((None, d_model, tn), lambda i, j: (layer, 0, j)),
            pl.BlockSpec((1, tn), lambda i, j: (0, j)),
        ],
        out_specs=pl.BlockSpec((tm, tn), lambda i, j: (i, j)),
        scratch_shapes=[pltpu.VMEM((tm, d_model), BF16)],
        compiler_params=_params("arbitrary", "arbitrary"),
        name="in_proj",
    )(x, g, w, scale)


def _conv_module_kernel(main_ref, halo_ref, cw_ref, cb_ref, lg_ref, lb_ref, o_ref,
                        g_ref, sh_ref, conv_ref, *, ts, rows):
    def glu(v):
        v = v.astype(F32)
        return v[:, :C_A] * jax.nn.sigmoid(v[:, C_A:])

    g_ref[0:HALO_A, :] = jnp.where(pl.program_id(0) > 0, glu(halo_ref[...]), 0.0)
    g_ref[HALO_A:, :] = glu(main_ref[...])
    sh_rows = sh_ref.shape[1]
    for k in range(1, SUBLANES):
        sh_ref[k - 1] = g_ref[k:k + sh_rows, :]

    first_tap = HALO_A - (CONV_A_WIDTH - 1)
    groups = rows // SUBLANES

    width = C_A // 2
    for cols in (slice(0, width), slice(width, C_A)):
        def chunk(r, carry, cols=cols):
            base = pl.multiple_of(r * rows, rows)
            acc = jnp.broadcast_to(cb_ref[:, cols].reshape(1, 1, width), (groups, SUBLANES, width))
            for j in range(CONV_A_WIDTH):
                k = (first_tap + j) % SUBLANES
                src = g_ref if k == 0 else sh_ref.at[k - 1]
                win = src[pl.ds(base + first_tap + j - k, rows), cols]
                acc = acc + cw_ref[j, :, cols][None] * win.reshape(groups, SUBLANES, width)
            conv_ref[pl.ds(base, rows), cols] = acc.reshape(rows, width)
            return carry

        lax.fori_loop(0, ts // rows, chunk, 0)

    u = conv_ref[...]
    xc = u - jnp.mean(u, axis=-1, keepdims=True)
    y = xc * lax.rsqrt(jnp.mean(xc * xc, axis=-1, keepdims=True) + NORM_EPS)
    y = y * lg_ref[...] + lb_ref[...]
    o_ref[...] = (y * jax.nn.sigmoid(y)).astype(o_ref.dtype)


def _conv_module(proj, cw, cb, lg, lb, *, ts, rows=4 * SUBLANES):
    s = proj.shape[0]
    halo_blocks = ts // HALO_A
    cw = jnp.broadcast_to(cw[:, None, :], (CONV_A_WIDTH, SUBLANES, C_A))
    return pl.pallas_call(
        functools.partial(_conv_module_kernel, ts=ts, rows=rows),
        out_shape=jax.ShapeDtypeStruct((s, C_A), BF16),
        grid=(s // ts,),
        in_specs=[
            pl.BlockSpec((ts, 2 * C_A), lambda i: (i, 0)),
            pl.BlockSpec((HALO_A, 2 * C_A), lambda i: (jnp.maximum(i * halo_blocks - 1, 0), 0)),
            pl.BlockSpec((CONV_A_WIDTH, SUBLANES, C_A), lambda i: (0, 0, 0)),
            pl.BlockSpec((1, C_A), lambda i: (0, 0)),
            pl.BlockSpec((1, C_A), lambda i: (0, 0)),
            pl.BlockSpec((1, C_A), lambda i: (0, 0)),
        ],
        out_specs=pl.BlockSpec((ts, C_A), lambda i: (i, 0)),
        scratch_shapes=[pltpu.VMEM((ts + HALO_A, C_A), F32),
                        pltpu.VMEM((SUBLANES - 1, ts + HALO_A - SUBLANES, C_A), F32),
                        pltpu.VMEM((ts, C_A), F32)],
        compiler_params=_params("arbitrary"),
        name="conv_module",
    )(proj, proj, cw, cb, lg, lb)


def _diff_attn_kernel(q_ref, k_ref, v_ref, bias_ref, lam_ref, g_ref, o_ref,
                      qs_ref, vt_ref, s_ref, m_ref, l_ref, acc_ref, *, tq, tk, lambda_init):
    qi = pl.program_id(1)
    s_len = v_ref.shape[0]
    ratio = tq // tk
    assert ratio % 2 == 0, "the static score double-buffering needs an even step count per tile"
    chunks =[slice(c * Q_CHUNK, (c + 1) * Q_CHUNK) for c in range(2 * tq // Q_CHUNK)]
    sub_tile = [(c * Q_CHUNK % tq) // tk for c in range(len(chunks))]
    sub_cols = [slice(c * Q_CHUNK % tk, c * Q_CHUNK % tk + Q_CHUNK) for c in range(len(chunks))]

    def load_queries(tile):
        q = q_ref[pl.ds(pl.multiple_of(tile * tq, tq), tq), :]
        lane = lax.broadcasted_iota(jnp.int32, q.shape, 1)
        zero = jnp.zeros_like(q)
        qs_ref[0:tq, :] = jnp.where(lane < DH_DIFF, q, zero)
        qs_ref[tq:, :] = jnp.where(lane >= DH_DIFF, q, zero)

    def scores(kb, c):
        return _dot_nt(k_ref[pl.ds(pl.multiple_of(kb * tk, tk), tk), :], qs_ref[chunks[c], :])

    @pl.when(qi == 0)
    def _():
        for c in range(s_len // VT_CHUNK):
            vt_ref[:, c * VT_CHUNK:(c + 1) * VT_CHUNK] = v_ref[c * VT_CHUNK:(c + 1) * VT_CHUNK, :].T
        load_queries(0)
        for c in range(len(chunks)):
            s_ref[0, c] = scores(0, c)

    m_ref[...] = jnp.full(m_ref.shape, NEG_INF, F32)
    l_ref[...] = jnp.zeros(l_ref.shape, F32)
    acc_ref[...] = jnp.zeros(acc_ref.shape, F32)

    def step(buf, kb, rel, next_kb, next_rel):
        vt_blk = vt_ref[:, pl.ds(pl.multiple_of(kb * tk, tk), tk)]
        wanted = [c for c in range(len(chunks)) if next_rel is None or sub_tile[c] - next_rel >= 0]

        def prefetch(i):
            if i < len(wanted):
                s_ref[1 - buf, wanted[i]] = scores(next_kb, wanted[i])

        for i in range(PREFETCH_AHEAD):
            prefetch(i)
        for c, cols in enumerate(chunks):
            prefetch(c + PREFETCH_AHEAD)
            back = None if rel is None else sub_tile[c] - rel
            if back is not None and back < 0:
                continue
            s = s_ref[buf, c]
            if back is not None and back <= 1:
                s = s + bias_ref[back, :, sub_cols[c]]
            m_prev = m_ref[:, cols]
            m_new = jnp.maximum(m_prev, jnp.max(s, axis=0, keepdims=True))
            alpha = jnp.exp2(m_prev - m_new)
            p = jnp.exp2(s - m_new)
            l_ref[:, cols] = alpha * l_ref[:, cols] + jnp.sum(p, axis=0, keepdims=True)
            acc_ref[:, cols] = alpha * acc_ref[:, cols] + _dot(vt_blk, p.astype(BF16))
            m_ref[:, cols] = m_new

    first = ratio * qi

    def far_pair(i, carry):
        step(0, 2 * i, None, 2 * i + 1, None)
        step(1, 2 * i + 1, None, 2 * i + 2, None)
        return carry

    lax.fori_loop(0, jnp.maximum((first - 2) // 2, 0), far_pair, 0)

    @pl.when(qi >= 1)
    def _():
        step(0, first - 2, None, first - 1, None)
        step(1, first - 1, -1, first, 0)

    for rel in range(ratio - 1):
        step(rel % 2, first + rel, rel, first + rel + 1, rel + 1)

    load_queries(jnp.minimum(qi + 1, pl.num_programs(1) - 1))
    step((ratio - 1) % 2, first + ratio - 1, ratio - 1, 0, None)

    lam_p = lam_ref[...]
    lam = (jnp.exp(jnp.sum(lam_p[0:1] * lam_p[1:2], axis=-1, keepdims=True))
           - jnp.exp(jnp.sum(lam_p[2:3] * lam_p[3:4], axis=-1, keepdims=True)) + lambda_init)
    o = acc_ref[...] / l_ref[...]
    d = o[:, 0:tq] - lam * o[:, tq:]
    d = d * lax.rsqrt(jnp.mean(d * d, axis=0, keepdims=True) + SUBLN_EPS)
    o_ref[...] = (d.T * g_ref[...] * (1.0 - lambda_init)).astype(o_ref.dtype)


def _diff_attn(proj, bias_tiles, lam_p, g, *, tq, tk, lambda_init):
    s = proj.shape[0]
    dv = 2 * DH_DIFF
    q_col = 2 * C_A // dv
    k_col = q_col + QK_WIDTH // dv
    v_col = k_col + QK_WIDTH // dv
    return pl.pallas_call(
        functools.partial(_diff_attn_kernel, tq=tq, tk=tk, lambda_init=lambda_init),
        out_shape=jax.ShapeDtypeStruct((s, V_WIDTH), BF16),
        grid=(H_DIFF, s // tq),
        in_specs=[
            pl.BlockSpec((s, dv), lambda h, i: (0, q_col + h)),
            pl.BlockSpec((s, dv), lambda h, i: (0, k_col + h)),
            pl.BlockSpec((s, dv), lambda h, i: (0, v_col + h)),
            pl.BlockSpec((None, 2, tk, tk), lambda h, i: (h, 0, 0, 0)),
            pl.BlockSpec((4, DH_DIFF), lambda h, i: (0, 0)),
            pl.BlockSpec((1, dv), lambda h, i: (0, 0)),
        ],
        out_specs=pl.BlockSpec((tq, dv), lambda h, i: (i, h)),
        scratch_shapes=[
            pltpu.VMEM((2 * tq, dv), BF16),
            pltpu.VMEM((dv, s), BF16),
            pltpu.VMEM((2, 2 * tq // Q_CHUNK, tk, Q_CHUNK), F32),
            pltpu.VMEM((1, 2 * tq), F32),
            pltpu.VMEM((1, 2 * tq), F32),
            pltpu.VMEM((dv, 2 * tq), F32),
        ],
        compiler_params=_params("arbitrary", "arbitrary"),
        name="diff_attn",
    )(proj, proj, proj, bias_tiles, lam_p, g)


def _mix_cross_kernel(a_ref, d_ref, x_ref, wo_ref, gc_ref, wcq_ref, ckv_ref, wco_ref, gf_ref,
                      xo_ref, hf_ref):
    x1 = x_ref[...] + _dot(a_ref[...], wo_ref[0:C_A, :]) + _dot(d_ref[...], wo_ref[C_A:, :])
    hc = _rms(x1, gc_ref[...], NORM_EPS).astype(BF16)
    cq = _dot(hc, wcq_ref[...]).astype(BF16)
    kv_off = H_CROSS * DH_CROSS
    heads = []
    for h in range(H_CROSS):
        lo, hi = h * DH_CROSS, (h + 1) * DH_CROSS
        cl = _dot_nt(cq[:, lo:hi], ckv_ref[:, lo:hi]) * (DH_CROSS ** -0.5)
        e = jnp.exp(cl - jnp.max(cl, axis=-1, keepdims=True))
        l = jnp.sum(e, axis=-1, keepdims=True)
        heads.append((_dot(e.astype(BF16), ckv_ref[:, kv_off + lo:kv_off + hi]) / l).astype(BF16))
    x2 = x1 + _dot(jnp.concatenate(heads, axis=1), wco_ref[...])
    xo_ref[...] = x2
    hf_ref[...] = _rms(x2, gf_ref[...], NORM_EPS).astype(hf_ref.dtype)


def _mix_cross(a, d, x, w_out, g_cross, w_cq, ckv, w_co, g_ffn, *, layer, tm):
    s, d_model = x.shape
    n_cq = w_cq.shape[2]
    const = lambda i: (0, 0)
    of_layer = lambda i: (layer, 0, 0)
    return pl.pallas_call(
        _mix_cross_kernel,
        out_shape=(jax.ShapeDtypeStruct((s, d_model), F32), jax.ShapeDtypeStruct((s, d_model), BF16)),
        grid=(s // tm,),
        in_specs=[
            pl.BlockSpec((tm, C_A), lambda i: (i, 0)),
            pl.BlockSpec((tm, V_WIDTH), lambda i: (i, 0)),
            pl.BlockSpec((tm, d_model), lambda i: (i, 0)),
            pl.BlockSpec((None, C_A + V_WIDTH, d_model), of_layer),
            pl.BlockSpec((1, d_model), const),
            pl.BlockSpec((None, d_model, n_cq), of_layer),
            pl.BlockSpec((None,) + ckv.shape[1:], of_layer),
            pl.BlockSpec((None, n_cq, d_model), of_layer),
            pl.BlockSpec((1, d_model), const),
        ],
        out_specs=(pl.BlockSpec((tm, d_model), lambda i: (i, 0)),
                   pl.BlockSpec((tm, d_model), lambda i: (i, 0))),
        compiler_params=_params("arbitrary"),
        name="mix_cross",
    )(a, d, x, w_out, g_cross, w_cq, ckv, w_co, g_ffn)


def _ffn_kernel(hf_ref, halo_ref, x_ref, wg_ref, wv_ref, cwg_ref, cwv_ref, cbg_ref, cbv_ref, wd_ref,
                o_ref, ext_ref, ug_ref, uv_ref, *, tm):
    @pl.when(pl.program_id(1) == 0)
    def _():
        halo = halo_ref[...]
        ext_ref[0:HALO_F, :] = jnp.where(pl.program_id(0) > 0, halo, jnp.zeros_like(halo))
        ext_ref[HALO_F:, :] = hf_ref[...]
        o_ref[...] = x_ref[...]

    ext = ext_ref[...]
    ug_ref[...] = _dot(ext, wg_ref[...])
    uv_ref[...] = _dot(ext, wv_ref[...])

    def conv(u_ref, cw_ref, cb_ref, row0, rows):
        out = cb_ref[...]
        for j in range(CONV_F_WIDTH):
            out = out + cw_ref[j:j + 1, :] * u_ref[pl.ds(HALO_F - (CONV_F_WIDTH - 1) + j + row0, rows), :]
        return out

    rows = tm // FFN_ROW_PIECES
    for r in range(FFN_ROW_PIECES):
        y = (jax.nn.silu(conv(ug_ref, cwg_ref, cbg_ref, r * rows, rows))
             * conv(uv_ref, cwv_ref, cbv_ref, r * rows, rows))
        o_ref[r * rows:(r + 1) * rows, :] += _dot(y.astype(BF16), wd_ref[...])


def _ffn(hf, x, w_up, cw, cb, w_down, *, layer, tm, tf):
    s, d_model = x.shape
    d_ff = w_down.shape[1]
    nf = d_ff // tf
    halo_blocks = tm // HALO_F
    return pl.pallas_call(
        functools.partial(_ffn_kernel, tm=tm),
        out_shape=jax.ShapeDtypeStruct((s, d_model), F32),
        grid=(s // tm, nf),
        in_specs=[
            pl.BlockSpec((tm, d_model), lambda i, f: (i, 0)),
            pl.BlockSpec((HALO_F, d_model), lambda i, f: (jnp.maximum(i * halo_blocks - 1, 0), 0)),
            pl.BlockSpec((tm, d_model), lambda i, f: (i, 0)),
            pl.BlockSpec((None, d_model, tf), lambda i, f: (layer, 0, f)),
            pl.BlockSpec((None, d_model, tf), lambda i, f: (layer, 0, nf + f)),
            pl.BlockSpec((CONV_F_WIDTH, tf), lambda i, f: (0, f)),
            pl.BlockSpec((CONV_F_WIDTH, tf), lambda i, f: (0, nf + f)),
            pl.BlockSpec((1, tf), lambda i, f: (0, f)),
            pl.BlockSpec((1, tf), lambda i, f: (0, nf + f)),
            pl.BlockSpec((None, tf, d_model), lambda i, f: (layer, f, 0)),
        ],
        out_specs=pl.BlockSpec((tm, d_model), lambda i, f: (i, 0)),
        scratch_shapes=[
            pltpu.VMEM((tm + HALO_F, d_model), BF16),
            pltpu.VMEM((tm + HALO_F, tf), F32),
            pltpu.VMEM((tm + HALO_F, tf), F32),
        ],
        compiler_params=_params("arbitrary", "arbitrary"),
        name="conv_ffn",
    )(hf, hf, x, w_up, w_up, cw, cw, cb, cb, w_down)


def _final_norm_kernel(x_ref, g_ref, o_ref):
    o_ref[...] = _rms(x_ref[...], g_ref[...], NORM_EPS)


def _final_norm(x, g, *, tm):
    s, d_model = x.shape
    return pl.pallas_call(
        _final_norm_kernel,
        out_shape=jax.ShapeDtypeStruct((s, d_model), F32),
        grid=(s // tm,),
        in_specs=[pl.BlockSpec((tm, d_model), lambda i: (i, 0)),
                  pl.BlockSpec((1, d_model), lambda i: (0, 0))],
        out_specs=pl.BlockSpec((tm, d_model), lambda i: (i, 0)),
        compiler_params=_params("arbitrary"),
        name="final_norm",
    )(x, g)


def kernel(x, mem, rel_bias_table, g_mix, w_in, conv_a_w, conv_a_b, ln_a_g, ln_a_b, diff_lambda, subln_g, w_out, g_cross, g_mem, w_cq, w_ckv, w_co, g_ffn, w_up, conv_f_w, conv_f_b, w_down, g_final):
    b, s, d_model = x.shape
    assert b == 1 and mem.shape[0] == 1
    depth = w_in.shape[0]
    d_ff = w_down.shape[1]

    tk_attn = _pick_tile(s, 512)
    tq_attn = _pick_tile(s, 1024)
    assert tk_attn >= MAX_DISTANCE, "the two biased key blocks must cover every bucketed distance"
    tm_proj = _pick_tile(s, 512)
    tn_proj = _pick_tile(w_in.shape[2], 1024)
    ts_conv = _pick_tile(s, 256)
    tm_mix = _pick_tile(s, 256)
    tm_ffn = _pick_tile(s, 512)
    tf_ffn = _pick_tile(d_ff, 512)

    row = lambda p: p.reshape(depth, 1, p.shape[-1])
    w_in_b, w_out_b, w_cq_b = w_in.astype(BF16), w_out.astype(BF16), w_cq.astype(BF16)
    w_ckv_b, w_co_b = w_ckv.astype(BF16), w_co.astype(BF16)
    w_up_b, w_down_b = w_up.astype(BF16), w_down.astype(BF16)
    g_mix, g_cross, g_ffn, g_mem = row(g_mix), row(g_cross), row(g_ffn), row(g_mem)
    conv_a_b, ln_a_g, ln_a_b = row(conv_a_b), row(ln_a_g), row(ln_a_b)
    subln_g, conv_f_b = row(subln_g), row(conv_f_b)

    bias_tiles = _bias_tiles(rel_bias_table, tk_attn)
    ckv = _mem_kv(mem[0], g_mem, w_ckv_b)

    xs = x[0]
    for l in range(depth):
        lambda_init = 0.8 - 0.6 * math.exp(-0.3 * l)
        proj = _in_proj(xs, g_mix[l], w_in_b, layer=l, tm=tm_proj, tn=tn_proj)
        a = _conv_module(proj, conv_a_w[l], conv_a_b[l], ln_a_g[l], ln_a_b[l], ts=ts_conv)
        d = _diff_attn(proj, bias_tiles, diff_lambda[l], subln_g[l], tq=tq_attn, tk=tk_attn,
                       lambda_init=lambda_init)
        xs, hf = _mix_cross(a, d, xs, w_out_b, g_cross[l], w_cq_b, ckv, w_co_b, g_ffn[l],
                            layer=l, tm=tm_mix)
        xs = _ffn(hf, xs, w_up_b, conv_f_w[l], conv_f_b[l], w_down_b, layer=l, tm=tm_ffn, tf=tf_ffn)
    return _final_norm(xs, g_final.reshape(1, d_model), tm=tm_proj)[None]
```

```python
import functools
import math

import jax
import jax.numpy as jnp
from jax import lax
from jax.experimental import pallas as pl
from jax.experimental.pallas import tpu as pltpu

N_MEM = 256
C_A = 1024
CONV_A_WIDTH = 31
H_DIFF = 8
DH_DIFF = 64
QK_WIDTH = 2 * H_DIFF * DH_DIFF
V_WIDTH = H_DIFF * 2 * DH_DIFF
N_BUCKETS = 32
MAX_DISTANCE = 128
H_CROSS = 4
DH_CROSS = 128
CONV_F_WIDTH = 3
NORM_EPS = 1e-6
SUBLN_EPS = 1e-5
NEG_INF = -1e30
LOG2E = math.log2(math.e)

LANES = 128
SUBLANES = 8
BF16_SUBLANES = 16
VMEM_LIMIT_BYTES = 56 * 1024 * 1024

F32 = jnp.float32
BF16 = jnp.bfloat16

HALO_A = 32
HALO_F = BF16_SUBLANES
VT_CHUNK = 256
Q_CHUNK = 256
FFN_ROW_PIECES = 4
PREFETCH_AHEAD = 2


def _params(*sem):
    return pltpu.CompilerParams(dimension_semantics=sem, vmem_limit_bytes=VMEM_LIMIT_BYTES)


def _rms(x, g, eps):
    return x * lax.rsqrt(jnp.mean(x * x, axis=-1, keepdims=True) + eps) * g


def _dot(a, b):
    return jnp.dot(a, b, preferred_element_type=F32)


def _dot_nt(a, b):
    return lax.dot_general(a, b, (((1,), (1,)), ((), ())), preferred_element_type=F32)


def _pick_tile(n, target):
    t = min(n, target)
    while n % t:
        t //= 2
    return t


def _bias_tiles_kernel(table_ref, o_ref, *, t):
    ki = lax.broadcasted_iota(jnp.int32, (t, t), 0)
    qi = lax.broadcasted_iota(jnp.int32, (t, t), 1)
    max_exact = N_BUCKETS // 2
    for j in range(2):
        n = qi - ki + j * t
        nf = jnp.maximum(n, 1).astype(F32)
        large = max_exact + (jnp.log(nf / max_exact) / math.log(MAX_DISTANCE / max_exact)
                             * (N_BUCKETS - max_exact)).astype(jnp.int32)
        large = jnp.minimum(large, N_BUCKETS - 1)
        bucket = jnp.where(n < max_exact, n, large)
        for h in range(H_DIFF):
            far = table_ref[N_BUCKETS - 1, h]
            b = jnp.zeros((t, t), F32)
            for k in range(N_BUCKETS - 1):
                b = jnp.where(bucket == k, (table_ref[k, h] - far) * LOG2E, b)
            if j == 0:
                b = jnp.where(n >= 0, b, NEG_INF)
            o_ref[h, j] = b


def _bias_tiles(table, t):
    return pl.pallas_call(
        functools.partial(_bias_tiles_kernel, t=t),
        out_shape=jax.ShapeDtypeStruct((H_DIFF, 2, t, t), F32),
        in_specs=[pl.BlockSpec(memory_space=pltpu.SMEM)],
        out_specs=pl.BlockSpec(memory_space=pltpu.VMEM),
        compiler_params=pltpu.CompilerParams(vmem_limit_bytes=VMEM_LIMIT_BYTES),
        name="bias_tiles",
    )(table)


def _mem_kv_kernel(mem_ref, g_ref, w_ref, o_ref):
    m = _rms(mem_ref[...], g_ref[...], NORM_EPS).astype(BF16)
    o_ref[...] = _dot(m, w_ref[...]).astype(o_ref.dtype)


def _mem_kv(mem, g_mem, w_ckv):
    depth, d_model, n_out = w_ckv.shape
    n_mem = mem.shape[0]
    return pl.pallas_call(
        _mem_kv_kernel,
        out_shape=jax.ShapeDtypeStruct((depth, n_mem, n_out), BF16),
        grid=(depth,),
        in_specs=[
            pl.BlockSpec((n_mem, d_model), lambda l: (0, 0)),
            pl.BlockSpec((None, 1, d_model), lambda l: (l, 0, 0)),
            pl.BlockSpec((None, d_model, n_out), lambda l: (l, 0, 0)),
        ],
        out_specs=pl.BlockSpec((None, n_mem, n_out), lambda l: (l, 0, 0)),
        compiler_params=_params("arbitrary"),
        name="mem_kv",
    )(mem, g_mem, w_ckv)


def _in_proj_kernel(x_ref, g_ref, w_ref, scale_ref, o_ref, h_ref):
    @pl.when(pl.program_id(1) == 0)
    def _():
        h_ref[...] = _rms(x_ref[...], g_ref[...], NORM_EPS).astype(h_ref.dtype)

    o_ref[...] = (_dot(h_ref[...], w_ref[...]) * scale_ref[...]).astype(o_ref.dtype)


def _in_proj(x, g, w, *, layer, tm, tn):
    s, d_model = x.shape
    n = w.shape[2]
    col = jnp.arange(n)
    is_q = jnp.logical_and(col >= 2 * C_A, col < 2 * C_A + QK_WIDTH)
    scale = jnp.where(is_q, DH_DIFF ** -0.5 * LOG2E, 1.0).astype(F32).reshape(1, n)
    return pl.pallas_call(
        _in_proj_kernel,
        out_shape=jax.ShapeDtypeStruct((s, n), BF16),
        grid=(s // tm, n // tn),
        in_specs=[
            pl.BlockSpec((tm, d_model), lambda i, j: (i, 0)),
            pl.BlockSpec((1, d_model), lambda i, j: (0, 0)),
            pl.BlockSpec((None, d_model, tn), lambda i, j: (layer, 0, j)),
            pl.BlockSpec((1, tn), lambda i, j: (0, j)),
        ],
        out_specs=pl.BlockSpec((tm, tn), lambda i, j: (i, j)),
        scratch_shapes=[pltpu.VMEM((tm, d_model), BF16)],
        compiler_params=_params("arbitrary", "arbitrary"),
        name="in_proj",
    )(x, g, w, scale)


def _conv_module_kernel(main_ref, halo_ref, cw_ref, cb_ref, lg_ref, lb_ref, o_ref,
                        g_ref, sh_ref, conv_ref, *, ts, rows):
    def glu(v):
        v = v.astype(F32)
        return v[:, :C_A] * jax.nn.sigmoid(v[:, C_A:])

    g_ref[0:HALO_A, :] = jnp.where(pl.program_id(0) > 0, glu(halo_ref[...]), 0.0)
    g_ref[HALO_A:, :] = glu(main_ref[...])
    sh_rows = sh_ref.shape[1]
    for k in range(1, SUBLANES):
        sh_ref[k - 1] = g_ref[k:k + sh_rows, :]

    first_tap = HALO_A - (CONV_A_WIDTH - 1)
    groups = rows // SUBLANES

    width = C_A // 2
    for cols in (slice(0, width), slice(width, C_A)):
        def chunk(r, carry, cols=cols):
            base = pl.multiple_of(r * rows, rows)
            acc = jnp.broadcast_to(cb_ref[:, cols].reshape(1, 1, width), (groups, SUBLANES, width))
            for j in range(CONV_A_WIDTH):
                k = (first_tap + j) % SUBLANES
                src = g_ref if k == 0 else sh_ref.at[k - 1]
                win = src[pl.ds(base + first_tap + j - k, rows), cols]
                acc = acc + cw_ref[j, :, cols][None] * win.reshape(groups, SUBLANES, width)
            conv_ref[pl.ds(base, rows), cols] = acc.reshape(rows, width)
            return carry

        lax.fori_loop(0, ts // rows, chunk, 0)

    u = conv_ref[...]
    xc = u - jnp.mean(u, axis=-1, keepdims=True)
    y = xc * lax.rsqrt(jnp.mean(xc * xc, axis=-1, keepdims=True) + NORM_EPS)
    y = y * lg_ref[...] + lb_ref[...]
    o_ref[...] = (y * jax.nn.sigmoid(y)).astype(o_ref.dtype)


def _conv_module(proj, cw, cb, lg, lb, *, ts, rows=4 * SUBLANES):
    s = proj.shape[0]
    halo_blocks = ts // HALO_A
    cw = jnp.broadcast_to(cw[:, None, :], (CONV_A_WIDTH, SUBLANES, C_A))
    return pl.pallas_call(
        functools.partial(_conv_module_kernel, ts=ts, rows=rows),
        out_shape=jax.ShapeDtypeStruct((s, C_A), BF16),
        grid=(s // ts,),
        in_specs=[
            pl.BlockSpec((ts, 2 * C_A), lambda i: (i, 0)),
            pl.BlockSpec((HALO_A, 2 * C_A), lambda i: (jnp.maximum(i * halo_blocks - 1, 0), 0)),
            pl.BlockSpec((CONV_A_WIDTH, SUBLANES, C_A), lambda i: (0, 0, 0)),
            pl.BlockSpec((1, C_A), lambda i: (0, 0)),
            pl.BlockSpec((1, C_A), lambda i: (0, 0)),
            pl.BlockSpec((1, C_A), lambda i: (0, 0)),
        ],
        out_specs=pl.BlockSpec((ts, C_A), lambda i: (i, 0)),
        scratch_shapes=[pltpu.VMEM((ts + HALO_A, C_A), F32),
                        pltpu.VMEM((SUBLANES - 1, ts + HALO_A - SUBLANES, C_A), F32),
                        pltpu.VMEM((ts, C_A), F32)],
        compiler_params=_params("arbitrary"),
        name="conv_module",
    )(proj, proj, cw, cb, lg, lb)


def _diff_attn_kernel(q_ref, k_ref, v_ref, bias_ref, lam_ref, g_ref, o_ref,
                      qs_ref, vt_ref, s_ref, m_ref, l_ref, acc_ref, *, tq, tk, lambda_init):
    qi = pl.program_id(1)
    s_len = v_ref.shape[0]
    ratio = tq // tk
    assert ratio % 2 == 0, "the static score double-buffering needs an even step count per tile"
    chunks =[slice(c * Q_CHUNK, (c + 1) * Q_CHUNK) for c in range(2 * tq // Q_CHUNK)]
    sub_tile = [(c * Q_CHUNK % tq) // tk for c in range(len(chunks))]
    sub_cols = [slice(c * Q_CHUNK % tk, c * Q_CHUNK % tk + Q_CHUNK) for c in range(len(chunks))]

    def load_queries(tile):
        q = q_ref[pl.ds(pl.multiple_of(tile * tq, tq), tq), :]
        lane = lax.broadcasted_iota(jnp.int32, q.shape, 1)
        zero = jnp.zeros_like(q)
        qs_ref[0:tq, :] = jnp.where(lane < DH_DIFF, q, zero)
        qs_ref[tq:, :] = jnp.where(lane >= DH_DIFF, q, zero)

    def scores(kb, c):
        return _dot_nt(k_ref[pl.ds(pl.multiple_of(kb * tk, tk), tk), :], qs_ref[chunks[c], :])

    @pl.when(qi == 0)
    def _():
        for c in range(s_len // VT_CHUNK):
            vt_ref[:, c * VT_CHUNK:(c + 1) * VT_CHUNK] = v_ref[c * VT_CHUNK:(c + 1) * VT_CHUNK, :].T
        load_queries(0)
        for c in range(len(chunks)):
            s_ref[0, c] = scores(0, c)

    m_ref[...] = jnp.full(m_ref.shape, NEG_INF, F32)
    l_ref[...] = jnp.zeros(l_ref.shape, F32)
    acc_ref[...] = jnp.zeros(acc_ref.shape, F32)

    def step(buf, kb, rel, next_kb, next_rel):
        vt_blk = vt_ref[:, pl.ds(pl.multiple_of(kb * tk, tk), tk)]
        wanted = [c for c in range(len(chunks)) if next_rel is None or sub_tile[c] - next_rel >= 0]

        def prefetch(i):
            if i < len(wanted):
                s_ref[1 - buf, wanted[i]] = scores(next_kb, wanted[i])

        for i in range(PREFETCH_AHEAD):
            prefetch(i)
        for c, cols in enumerate(chunks):
            prefetch(c + PREFETCH_AHEAD)
            back = None if rel is None else sub_tile[c] - rel
            if back is not None and back < 0:
                continue
            s = s_ref[buf, c]
            if back is not None and back <= 1:
                s = s + bias_ref[back, :, sub_cols[c]]
            m_prev = m_ref[:, cols]
            m_new = jnp.maximum(m_prev, jnp.max(s, axis=0, keepdims=True))
            alpha = jnp.exp2(m_prev - m_new)
            p = jnp.exp2(s - m_new)
            l_ref[:, cols] = alpha * l_ref[:, cols] + jnp.sum(p, axis=0, keepdims=True)
            acc_ref[:, cols] = alpha * acc_ref[:, cols] + _dot(vt_blk, p.astype(BF16))
            m_ref[:, cols] = m_new

    first = ratio * qi

    def far_pair(i, carry):
        step(0, 2 * i, None, 2 * i + 1, None)
        step(1, 2 * i + 1, None, 2 * i + 2, None)
        return carry

    lax.fori_loop(0, jnp.maximum((first - 2) // 2, 0), far_pair, 0)

    @pl.when(qi >= 1)
    def _():
        step(0, first - 2, None, first - 1, None)
        step(1, first - 1, -1, first, 0)

    for rel in range(ratio - 1):
        step(rel % 2, first + rel, rel, first + rel + 1, rel + 1)

    load_queries(jnp.minimum(qi + 1, pl.num_programs(1) - 1))
    step((ratio - 1) % 2, first + ratio - 1, ratio - 1, 0, None)

    lam_p = lam_ref[...]
    lam = (jnp.exp(jnp.sum(lam_p[0:1] * lam_p[1:2], axis=-1, keepdims=True))
           - jnp.exp(jnp.sum(lam_p[2:3] * lam_p[3:4], axis=-1, keepdims=True)) + lambda_init)
    o = acc_ref[...] / l_ref[...]
    d = o[:, 0:tq] - lam * o[:, tq:]
    d = d * lax.rsqrt(jnp.mean(d * d, axis=0, keepdims=True) + SUBLN_EPS)
    o_ref[...] = (d.T * g_ref[...] * (1.0 - lambda_init)).astype(o_ref.dtype)


def _diff_attn(proj, bias_tiles, lam_p, g, *, tq, tk, lambda_init):
    s = proj.shape[0]
    dv = 2 * DH_DIFF
    q_col = 2 * C_A // dv
    k_col = q_col + QK_WIDTH // dv
    v_col = k_col + QK_WIDTH // dv
    return pl.pallas_call(
        functools.partial(_diff_attn_kernel, tq=tq, tk=tk, lambda_init=lambda_init),
        out_shape=jax.ShapeDtypeStruct((s, V_WIDTH), BF16),
        grid=(H_DIFF, s // tq),
        in_specs=[
            pl.BlockSpec((s, dv), lambda h, i: (0, q_col + h)),
            pl.BlockSpec((s, dv), lambda h, i: (0, k_col + h)),
            pl.BlockSpec((s, dv), lambda h, i: (0, v_col + h)),
            pl.BlockSpec((None, 2, tk, tk), lambda h, i: (h, 0, 0, 0)),
            pl.BlockSpec((4, DH_DIFF), lambda h, i: (0, 0)),
            pl.BlockSpec((1, dv), lambda h, i: (0, 0)),
        ],
        out_specs=pl.BlockSpec((tq, dv), lambda h, i: (i, h)),
        scratch_shapes=[
            pltpu.VMEM((2 * tq, dv), BF16),
            pltpu.VMEM((dv, s), BF16),
            pltpu.VMEM((2, 2 * tq // Q_CHUNK, tk, Q_CHUNK), F32),
            pltpu.VMEM((1, 2 * tq), F32),
            pltpu.VMEM((1, 2 * tq), F32),
            pltpu.VMEM((dv, 2 * tq), F32),
        ],
        compiler_params=_params("arbitrary", "arbitrary"),
        name="diff_attn",
    )(proj, proj, proj, bias_tiles, lam_p, g)


def _mix_cross_kernel(a_ref, d_ref, x_ref, wo_ref, gc_ref, wcq_ref, ckv_ref, wco_ref, gf_ref,
                      xo_ref, hf_ref):
    x1 = x_ref[...] + _dot(a_ref[...], wo_ref[0:C_A, :]) + _dot(d_ref[...], wo_ref[C_A:, :])
    hc = _rms(x1, gc_ref[...], NORM_EPS).astype(BF16)
    cq = _dot(hc, wcq_ref[...]).astype(BF16)
    kv_off = H_CROSS * DH_CROSS
    heads = []
    for h in range(H_CROSS):
        lo, hi = h * DH_CROSS, (h + 1) * DH_CROSS
        cl = _dot_nt(cq[:, lo:hi], ckv_ref[:, lo:hi]) * (DH_CROSS ** -0.5)
        e = jnp.exp(cl - jnp.max(cl, axis=-1, keepdims=True))
        l = jnp.sum(e, axis=-1, keepdims=True)
        heads.append((_dot(e.astype(BF16), ckv_ref[:, kv_off + lo:kv_off + hi]) / l).astype(BF16))
    x2 = x1 + _dot(jnp.concatenate(heads, axis=1), wco_ref[...])
    xo_ref[...] = x2
    hf_ref[...] = _rms(x2, gf_ref[...], NORM_EPS).astype(hf_ref.dtype)


def _mix_cross(a, d, x, w_out, g_cross, w_cq, ckv, w_co, g_ffn, *, layer, tm):
    s, d_model = x.shape
    n_cq = w_cq.shape[2]
    const = lambda i: (0, 0)
    of_layer = lambda i: (layer, 0, 0)
    resident = pl.Buffered(1)
    return pl.pallas_call(
        _mix_cross_kernel,
        out_shape=(jax.ShapeDtypeStruct((s, d_model), F32), jax.ShapeDtypeStruct((s, d_model), BF16)),
        grid=(s // tm,),
        in_specs=[
            pl.BlockSpec((tm, C_A), lambda i: (i, 0)),
            pl.BlockSpec((tm, V_WIDTH), lambda i: (i, 0)),
            pl.BlockSpec((tm, d_model), lambda i: (i, 0)),
            pl.BlockSpec((None, C_A + V_WIDTH, d_model), of_layer, pipeline_mode=resident),
            pl.BlockSpec((1, d_model), const),
            pl.BlockSpec((None, d_model, n_cq), of_layer, pipeline_mode=resident),
            pl.BlockSpec((None,) + ckv.shape[1:], of_layer, pipeline_mode=resident),
            pl.BlockSpec((None, n_cq, d_model), of_layer, pipeline_mode=resident),
            pl.BlockSpec((1, d_model), const),
        ],
        out_specs=(pl.BlockSpec((tm, d_model), lambda i: (i, 0)),
                   pl.BlockSpec((tm, d_model), lambda i: (i, 0))),
        compiler_params=_params("arbitrary"),
        name="mix_cross",
    )(a, d, x, w_out, g_cross, w_cq, ckv, w_co, g_ffn)


def _ffn_kernel(hf_ref, halo_ref, x_ref, wg_ref, wv_ref, cwg_ref, cwv_ref, cbg_ref, cbv_ref, wd_ref,
                gout_ref, o_ref, ext_ref, ug_ref, uv_ref, *, tm, norm_output):
    @pl.when(pl.program_id(1) == 0)
    def _():
        halo = halo_ref[...]
        ext_ref[0:HALO_F, :] = jnp.where(pl.program_id(0) > 0, halo, jnp.zeros_like(halo))
        ext_ref[HALO_F:, :] = hf_ref[...]
        o_ref[...] = x_ref[...]

    ext = ext_ref[...]
    ug_ref[...] = _dot(ext, wg_ref[...])
    uv_ref[...] = _dot(ext, wv_ref[...])

    def conv(u_ref, cw_ref, cb_ref, row0, rows):
        out = cb_ref[...]
        for j in range(CONV_F_WIDTH):
            out = out + cw_ref[j:j + 1, :] * u_ref[pl.ds(HALO_F - (CONV_F_WIDTH - 1) + j + row0, rows), :]
        return out

    rows = tm // FFN_ROW_PIECES
    for r in range(FFN_ROW_PIECES):
        y = (jax.nn.silu(conv(ug_ref, cwg_ref, cbg_ref, r * rows, rows))
             * conv(uv_ref, cwv_ref, cbv_ref, r * rows, rows))
        o_ref[r * rows:(r + 1) * rows, :] += _dot(y.astype(BF16), wd_ref[...])

    if norm_output:
        @pl.when(pl.program_id(1) == pl.num_programs(1) - 1)
        def _():
            o_ref[...] = _rms(o_ref[...], gout_ref[...], NORM_EPS)


def _ffn(hf, x, w_up, cw, cb, w_down, g_out, *, layer, tm, tf, norm_output):
    s, d_model = x.shape
    d_ff = w_down.shape[1]
    nf = d_ff // tf
    halo_blocks = tm // HALO_F
    return pl.pallas_call(
        functools.partial(_ffn_kernel, tm=tm, norm_output=norm_output),
        out_shape=jax.ShapeDtypeStruct((s, d_model), F32),
        grid=(s // tm, nf),
        in_specs=[
            pl.BlockSpec((tm, d_model), lambda i, f: (i, 0)),
            pl.BlockSpec((HALO_F, d_model), lambda i, f: (jnp.maximum(i * halo_blocks - 1, 0), 0)),
            pl.BlockSpec((tm, d_model), lambda i, f: (i, 0)),
            pl.BlockSpec((None, d_model, tf), lambda i, f: (layer, 0, f)),
            pl.BlockSpec((None, d_model, tf), lambda i, f: (layer, 0, nf + f)),
            pl.BlockSpec((CONV_F_WIDTH, tf), lambda i, f: (0, f)),
            pl.BlockSpec((CONV_F_WIDTH, tf), lambda i, f: (0, nf + f)),
            pl.BlockSpec((1, tf), lambda i, f: (0, f)),
            pl.BlockSpec((1, tf), lambda i, f: (0, nf + f)),
            pl.BlockSpec((None, tf, d_model), lambda i, f: (layer, f, 0)),
            pl.BlockSpec((1, d_model), lambda i, f: (0, 0)),
        ],
        out_specs=pl.BlockSpec((tm, d_model), lambda i, f: (i, 0)),
        scratch_shapes=[
            pltpu.VMEM((tm + HALO_F, d_model), BF16),
            pltpu.VMEM((tm + HALO_F, tf), F32),
            pltpu.VMEM((tm + HALO_F, tf), F32),
        ],
        compiler_params=_params("arbitrary", "arbitrary"),
        name="conv_ffn",
    )(hf, hf, x, w_up, w_up, cw, cw, cb, cb, w_down, g_out)


def kernel(x, mem, rel_bias_table, g_mix, w_in, conv_a_w, conv_a_b, ln_a_g, ln_a_b, diff_lambda, subln_g, w_out, g_cross, g_mem, w_cq, w_ckv, w_co, g_ffn, w_up, conv_f_w, conv_f_b, w_down, g_final):
    b, s, d_model = x.shape
    assert b == 1 and mem.shape[0] == 1
    depth = w_in.shape[0]
    d_ff = w_down.shape[1]

    tk_attn = _pick_tile(s, 512)
    tq_attn = _pick_tile(s, 1024)
    assert tk_attn >= MAX_DISTANCE, "the two biased key blocks must cover every bucketed distance"
    tm_proj = _pick_tile(s, 1024)
    tn_proj = _pick_tile(w_in.shape[2], 1024)
    ts_conv = _pick_tile(s, 256)
    tm_mix = _pick_tile(s, 512)
    tm_ffn = _pick_tile(s, 512)
    tf_ffn = _pick_tile(d_ff, 512)

    row = lambda p: p.reshape(depth, 1, p.shape[-1])
    w_in_b, w_out_b, w_cq_b = w_in.astype(BF16), w_out.astype(BF16), w_cq.astype(BF16)
    w_ckv_b, w_co_b = w_ckv.astype(BF16), w_co.astype(BF16)
    w_up_b, w_down_b = w_up.astype(BF16), w_down.astype(BF16)
    g_mix, g_cross, g_ffn, g_mem = row(g_mix), row(g_cross), row(g_ffn), row(g_mem)
    conv_a_b, ln_a_g, ln_a_b = row(conv_a_b), row(ln_a_g), row(ln_a_b)
    subln_g, conv_f_b = row(subln_g), row(conv_f_b)

    bias_tiles = _bias_tiles(rel_bias_table, tk_attn)
    ckv = _mem_kv(mem[0], g_mem, w_ckv_b)

    xs = x[0]
    for l in range(depth):
        lambda_init = 0.8 - 0.6 * math.exp(-0.3 * l)
        proj = _in_proj(xs, g_mix[l], w_in_b, layer=l, tm=tm_proj, tn=tn_proj)
        a = _conv_module(proj, conv_a_w[l], conv_a_b[l], ln_a_g[l], ln_a_b[l], ts=ts_conv)
        d = _diff_attn(proj, bias_tiles, diff_lambda[l], subln_g[l], tq=tq_attn, tk=tk_attn,
                       lambda_init=lambda_init)
        xs, hf = _mix_cross(a, d, xs, w_out_b, g_cross[l], w_cq_b, ckv, w_co_b, g_ffn[l],
                            layer=l, tm=tm_mix)
        xs = _ffn(hf, xs, w_up_b, conv_f_w[l], conv_f_b[l], w_down_b, g_final.reshape(1, d_model),
                  layer=l, tm=tm_ffn, tf=tf_ffn, norm_output=(l == depth - 1))
    return xs[None]
```

```python
import functools
import math

import jax
import jax.numpy as jnp
from jax import lax
from jax.experimental import pallas as pl
from jax.experimental.pallas import tpu as pltpu

N_MEM = 256
C_A = 1024
CONV_A_WIDTH = 31
H_DIFF = 8
DH_DIFF = 64
QK_WIDTH = 2 * H_DIFF * DH_DIFF
V_WIDTH = H_DIFF * 2 * DH_DIFF
N_BUCKETS = 32
MAX_DISTANCE = 128
H_CROSS = 4
DH_CROSS = 128
CONV_F_WIDTH = 3
NORM_EPS = 1e-6
SUBLN_EPS = 1e-5
NEG_INF = -1e30
LOG2E = math.log2(math.e)

LANES = 128
SUBLANES = 8
BF16_SUBLANES = 16
VMEM_LIMIT_BYTES = 56 * 1024 * 1024

F32 = jnp.float32
BF16 = jnp.bfloat16

HALO_A = 32
HALO_F = BF16_SUBLANES
VT_CHUNK = 256
Q_CHUNK = 256
CONV_LANES = 512
FFN_ROW_PIECES = 2
PREFETCH_AHEAD = 2


def _params(*sem):
    return pltpu.CompilerParams(dimension_semantics=sem, vmem_limit_bytes=VMEM_LIMIT_BYTES)


def _rms(x, g, eps):
    return x * lax.rsqrt(jnp.mean(x * x, axis=-1, keepdims=True) + eps) * g


def _dot(a, b):
    return jnp.dot(a, b, preferred_element_type=F32)


def _dot_nt(a, b):
    return lax.dot_general(a, b, (((1,), (1,)), ((), ())), preferred_element_type=F32)


def _pick_tile(n, target):
    t = min(n, target)
    while n % t:
        t //= 2
    return t


def _bias_tiles_kernel(table_ref, o_ref, *, t):
    ki = lax.broadcasted_iota(jnp.int32, (t, t), 0)
    qi = lax.broadcasted_iota(jnp.int32, (t, t), 1)
    max_exact = N_BUCKETS // 2
    for j in range(2):
        n = qi - ki + j * t
        nf = jnp.maximum(n, 1).astype(F32)
        large = max_exact + (jnp.log(nf / max_exact) / math.log(MAX_DISTANCE / max_exact)
                             * (N_BUCKETS - max_exact)).astype(jnp.int32)
        large = jnp.minimum(large, N_BUCKETS - 1)
        bucket = jnp.where(n < max_exact, n, large)
        for h in range(H_DIFF):
            far = table_ref[N_BUCKETS - 1, h]
            b = jnp.zeros((t, t), F32)
            for k in range(N_BUCKETS - 1):
                b = jnp.where(bucket == k, (table_ref[k, h] - far) * LOG2E, b)
            if j == 0:
                b = jnp.where(n >= 0, b, NEG_INF)
            o_ref[h, j] = b


def _bias_tiles(table, t):
    return pl.pallas_call(
        functools.partial(_bias_tiles_kernel, t=t),
        out_shape=jax.ShapeDtypeStruct((H_DIFF, 2, t, t), F32),
        in_specs=[pl.BlockSpec(memory_space=pltpu.SMEM)],
        out_specs=pl.BlockSpec(memory_space=pltpu.VMEM),
        compiler_params=pltpu.CompilerParams(vmem_limit_bytes=VMEM_LIMIT_BYTES),
        name="bias_tiles",
    )(table)


def _mem_kv_kernel(mem_ref, g_ref, w_ref, o_ref):
    m = _rms(mem_ref[...], g_ref[...], NORM_EPS).astype(BF16)
    o_ref[...] = _dot(m, w_ref[...]).astype(o_ref.dtype)


def _mem_kv(mem, g_mem, w_ckv):
    depth, d_model, n_out = w_ckv.shape
    n_mem = mem.shape[0]
    return pl.pallas_call(
        _mem_kv_kernel,
        out_shape=jax.ShapeDtypeStruct((depth, n_mem, n_out), BF16),
        grid=(depth,),
        in_specs=[
            pl.BlockSpec((n_mem, d_model), lambda l: (0, 0)),
            pl.BlockSpec((None, 1, d_model), lambda l: (l, 0, 0)),
            pl.BlockSpec((None, d_model, n_out), lambda l: (l, 0, 0)),
        ],
        out_specs=pl.BlockSpec((None, n_mem, n_out), lambda l: (l, 0, 0)),
        compiler_params=_params("arbitrary"),
        name="mem_kv",
    )(mem, g_mem, w_ckv)


def _in_proj_kernel(x_ref, g_ref, w_ref, scale_ref, o_ref, h_ref):
    @pl.when(pl.program_id(1) == 0)
    def _():
        h_ref[...] = _rms(x_ref[...], g_ref[...], NORM_EPS).astype(h_ref.dtype)

    o_ref[...] = (_dot(h_ref[...], w_ref[...]) * scale_ref[...]).astype(o_ref.dtype)


def _in_proj(x, g, w, *, layer, tm, tn):
    s, d_model = x.shape
    n = w.shape[2]
    col = jnp.arange(n)
    is_q = jnp.logical_and(col >= 2 * C_A, col < 2 * C_A + QK_WIDTH)
    scale = jnp.where(is_q, DH_DIFF ** -0.5 * LOG2E, 1.0).astype(F32).reshape(1, n)
    return pl.pallas_call(
        _in_proj_kernel,
        out_shape=jax.ShapeDtypeStruct((s, n), BF16),
        grid=(s // tm, n // tn),
        in_specs=[
            pl.BlockSpec((tm, d_model), lambda i, j: (i, 0)),
            pl.BlockSpec((1, d_model), lambda i, j: (0, 0)),
            pl.BlockSpec((None, d_model, tn), lambda i, j: (layer, 0, j)),
            pl.BlockSpec((1, tn), lambda i, j: (0, j)),
        ],
        out_specs=pl.BlockSpec((tm, tn), lambda i, j: (i, j)),
        scratch_shapes=[pltpu.VMEM((tm, d_model), BF16)],
        compiler_params=_params("arbitrary", "arbitrary"),
        name="in_proj",
    )(x, g, w, scale)


def _conv_module_kernel(main_ref, halo_ref, cw_ref, cb_ref, lg_ref, lb_ref, o_ref,
                        g_ref, sh_ref, conv_ref, *, ts, rows):
    def glu(v):
        v = v.astype(F32)
        return v[:, :C_A] * jax.nn.sigmoid(v[:, C_A:])

    g_ref[0:HALO_A, :] = jnp.where(pl.program_id(0) > 0, glu(halo_ref[...]), 0.0)
    g_ref[HALO_A:, :] = glu(main_ref[...])
    sh_rows = sh_ref.shape[1]
    for k in range(1, SUBLANES):
        sh_ref[k - 1] = g_ref[k:k + sh_rows, :]

    first_tap = HALO_A - (CONV_A_WIDTH - 1)
    groups = rows // SUBLANES

    width = CONV_LANES
    for cols in [slice(lo, lo + width) for lo in range(0, C_A, width)]:
        def chunk(r, carry, cols=cols):
            base = pl.multiple_of(r * rows, rows)
            acc = jnp.broadcast_to(cb_ref[:, cols].reshape(1, 1, width), (groups, SUBLANES, width))
            for j in range(CONV_A_WIDTH):
                k = (first_tap + j) % SUBLANES
                src = g_ref if k == 0 else sh_ref.at[k - 1]
                win = src[pl.ds(base + first_tap + j - k, rows), cols]
                acc = acc + cw_ref[j, :, cols][None] * win.reshape(groups, SUBLANES, width)
            conv_ref[pl.ds(base, rows), cols] = acc.reshape(rows, width)
            return carry

        lax.fori_loop(0, ts // rows, chunk, 0)

    u = conv_ref[...]
    xc = u - jnp.mean(u, axis=-1, keepdims=True)
    y = xc * lax.rsqrt(jnp.mean(xc * xc, axis=-1, keepdims=True) + NORM_EPS)
    y = y * lg_ref[...] + lb_ref[...]
    o_ref[...] = (y * jax.nn.sigmoid(y)).astype(o_ref.dtype)


def _conv_module(proj, cw, cb, lg, lb, *, ts, rows=4 * SUBLANES):
    s = proj.shape[0]
    halo_blocks = ts // HALO_A
    cw = jnp.broadcast_to(cw[:, None, :], (CONV_A_WIDTH, SUBLANES, C_A))
    return pl.pallas_call(
        functools.partial(_conv_module_kernel, ts=ts, rows=rows),
        out_shape=jax.ShapeDtypeStruct((s, C_A), BF16),
        grid=(s // ts,),
        in_specs=[
            pl.BlockSpec((ts, 2 * C_A), lambda i: (i, 0)),
            pl.BlockSpec((HALO_A, 2 * C_A), lambda i: (jnp.maximum(i * halo_blocks - 1, 0), 0)),
            pl.BlockSpec((CONV_A_WIDTH, SUBLANES, C_A), lambda i: (0, 0, 0)),
            pl.BlockSpec((1, C_A), lambda i: (0, 0)),
            pl.BlockSpec((1, C_A), lambda i: (0, 0)),
            pl.BlockSpec((1, C_A), lambda i: (0, 0)),
        ],
        out_specs=pl.BlockSpec((ts, C_A), lambda i: (i, 0)),
        scratch_shapes=[pltpu.VMEM((ts + HALO_A, C_A), F32),
                        pltpu.VMEM((SUBLANES - 1, ts + HALO_A - SUBLANES, C_A), F32),
                        pltpu.VMEM((ts, C_A), F32)],
        compiler_params=_params("arbitrary"),
        name="conv_module",
    )(proj, proj, cw, cb, lg, lb)


def _diff_attn_kernel(q_ref, k_ref, v_ref, bias_ref, lam_ref, g_ref, o_ref,
                      qs_ref, vt_ref, s_ref, m_ref, l_ref, acc_ref, *, tq, tk, lambda_init):
    qi = pl.program_id(1)
    s_len = v_ref.shape[0]
    ratio = tq // tk
    assert ratio % 2 == 0, "the static score double-buffering needs an even step count per tile"
    chunks =[slice(c * Q_CHUNK, (c + 1) * Q_CHUNK) for c in range(2 * tq // Q_CHUNK)]
    sub_tile = [(c * Q_CHUNK % tq) // tk for c in range(len(chunks))]
    sub_cols = [slice(c * Q_CHUNK % tk, c * Q_CHUNK % tk + Q_CHUNK) for c in range(len(chunks))]

    def load_queries(tile):
        q = q_ref[pl.ds(pl.multiple_of(tile * tq, tq), tq), :]
        lane = lax.broadcasted_iota(jnp.int32, q.shape, 1)
        zero = jnp.zeros_like(q)
        qs_ref[0:tq, :] = jnp.where(lane < DH_DIFF, q, zero)
        qs_ref[tq:, :] = jnp.where(lane >= DH_DIFF, q, zero)

    def scores(kb, c):
        return _dot_nt(k_ref[pl.ds(pl.multiple_of(kb * tk, tk), tk), :], qs_ref[chunks[c], :])

    @pl.when(qi == 0)
    def _():
        for c in range(s_len // VT_CHUNK):
            vt_ref[:, c * VT_CHUNK:(c + 1) * VT_CHUNK] = v_ref[c * VT_CHUNK:(c + 1) * VT_CHUNK, :].T
        load_queries(0)
        for c in range(len(chunks)):
            s_ref[0, c] = scores(0, c)

    m_ref[...] = jnp.full(m_ref.shape, NEG_INF, F32)
    l_ref[...] = jnp.zeros(l_ref.shape, F32)
    acc_ref[...] = jnp.zeros(acc_ref.shape, F32)

    def step(buf, kb, rel, next_kb, next_rel):
        vt_blk = vt_ref[:, pl.ds(pl.multiple_of(kb * tk, tk), tk)]
        wanted = [c for c in range(len(chunks)) if next_rel is None or sub_tile[c] - next_rel >= 0]

        def prefetch(i):
            if i < len(wanted):
                s_ref[1 - buf, wanted[i]] = scores(next_kb, wanted[i])

        for i in range(PREFETCH_AHEAD):
            prefetch(i)
        for c, cols in enumerate(chunks):
            prefetch(c + PREFETCH_AHEAD)
            back = None if rel is None else sub_tile[c] - rel
            if back is not None and back < 0:
                continue
            s = s_ref[buf, c]
            if back is not None and back <= 1:
                s = s + bias_ref[back, :, sub_cols[c]]
            m_prev = m_ref[:, cols]
            m_new = jnp.maximum(m_prev, jnp.max(s, axis=0, keepdims=True))
            alpha = jnp.exp2(m_prev - m_new)
            p = jnp.exp2(s - m_new)
            l_ref[:, cols] = alpha * l_ref[:, cols] + jnp.sum(p, axis=0, keepdims=True)
            acc_ref[:, cols] = alpha * acc_ref[:, cols] + _dot(vt_blk, p.astype(BF16))
            m_ref[:, cols] = m_new

    first = ratio * qi

    def far_pair(i, carry):
        step(0, 2 * i, None, 2 * i + 1, None)
        step(1, 2 * i + 1, None, 2 * i + 2, None)
        return carry

    lax.fori_loop(0, jnp.maximum((first - 2) // 2, 0), far_pair, 0)

    @pl.when(qi >= 1)
    def _():
        step(0, first - 2, None, first - 1, None)
        step(1, first - 1, -1, first, 0)

    for rel in range(ratio - 1):
        step(rel % 2, first + rel, rel, first + rel + 1, rel + 1)

    load_queries(jnp.minimum(qi + 1, pl.num_programs(1) - 1))
    step((ratio - 1) % 2, first + ratio - 1, ratio - 1, 0, None)

    lam_p = lam_ref[...]
    lam = (jnp.exp(jnp.sum(lam_p[0:1] * lam_p[1:2], axis=-1, keepdims=True))
           - jnp.exp(jnp.sum(lam_p[2:3] * lam_p[3:4], axis=-1, keepdims=True)) + lambda_init)
    o = acc_ref[...] / l_ref[...]
    d = o[:, 0:tq] - lam * o[:, tq:]
    d = d * lax.rsqrt(jnp.mean(d * d, axis=0, keepdims=True) + SUBLN_EPS)
    o_ref[...] = (d.T * g_ref[...] * (1.0 - lambda_init)).astype(o_ref.dtype)


def _diff_attn(proj, bias_tiles, lam_p, g, *, tq, tk, lambda_init):
    s = proj.shape[0]
    dv = 2 * DH_DIFF
    q_col = 2 * C_A // dv
    k_col = q_col + QK_WIDTH // dv
    v_col = k_col + QK_WIDTH // dv
    return pl.pallas_call(
        functools.partial(_diff_attn_kernel, tq=tq, tk=tk, lambda_init=lambda_init),
        out_shape=jax.ShapeDtypeStruct((s, V_WIDTH), BF16),
        grid=(H_DIFF, s // tq),
        in_specs=[
            pl.BlockSpec((s, dv), lambda h, i: (0, q_col + h)),
            pl.BlockSpec((s, dv), lambda h, i: (0, k_col + h)),
            pl.BlockSpec((s, dv), lambda h, i: (0, v_col + h)),
            pl.BlockSpec((None, 2, tk, tk), lambda h, i: (h, 0, 0, 0)),
            pl.BlockSpec((4, DH_DIFF), lambda h, i: (0, 0)),
            pl.BlockSpec((1, dv), lambda h, i: (0, 0)),
        ],
        out_specs=pl.BlockSpec((tq, dv), lambda h, i: (i, h)),
        scratch_shapes=[
            pltpu.VMEM((2 * tq, dv), BF16),
            pltpu.VMEM((dv, s), BF16),
            pltpu.VMEM((2, 2 * tq // Q_CHUNK, tk, Q_CHUNK), F32),
            pltpu.VMEM((1, 2 * tq), F32),
            pltpu.VMEM((1, 2 * tq), F32),
            pltpu.VMEM((dv, 2 * tq), F32),
        ],
        compiler_params=_params("arbitrary", "arbitrary"),
        name="diff_attn",
    )(proj, proj, proj, bias_tiles, lam_p, g)


def _mix_cross_kernel(a_ref, d_ref, x_ref, wo_ref, gc_ref, wcq_ref, ckv_ref, wco_ref, gf_ref,
                      xo_ref, hf_ref):
    x1 = x_ref[...] + _dot(a_ref[...], wo_ref[0:C_A, :]) + _dot(d_ref[...], wo_ref[C_A:, :])
    hc = _rms(x1, gc_ref[...], NORM_EPS).astype(BF16)
    cq = _dot(hc, wcq_ref[...]).astype(BF16)
    kv_off = H_CROSS * DH_CROSS
    heads = []
    for h in range(H_CROSS):
        lo, hi = h * DH_CROSS, (h + 1) * DH_CROSS
        cl = _dot_nt(cq[:, lo:hi], ckv_ref[:, lo:hi]) * (DH_CROSS ** -0.5)
        e = jnp.exp(cl - jnp.max(cl, axis=-1, keepdims=True))
        l = jnp.sum(e, axis=-1, keepdims=True)
        heads.append((_dot(e.astype(BF16), ckv_ref[:, kv_off + lo:kv_off + hi]) / l).astype(BF16))
    x2 = x1 + _dot(jnp.concatenate(heads, axis=1), wco_ref[...])
    xo_ref[...] = x2
    hf_ref[...] = _rms(x2, gf_ref[...], NORM_EPS).astype(hf_ref.dtype)


def _mix_cross(a, d, x, w_out, g_cross, w_cq, ckv, w_co, g_ffn, *, layer, tm):
    s, d_model = x.shape
    n_cq = w_cq.shape[2]
    const = lambda i: (0, 0)
    of_layer = lambda i: (layer, 0, 0)
    resident = pl.Buffered(1)
    return pl.pallas_call(
        _mix_cross_kernel,
        out_shape=(jax.ShapeDtypeStruct((s, d_model), F32), jax.ShapeDtypeStruct((s, d_model), BF16)),
        grid=(s // tm,),
        in_specs=[
            pl.BlockSpec((tm, C_A), lambda i: (i, 0)),
            pl.BlockSpec((tm, V_WIDTH), lambda i: (i, 0)),
            pl.BlockSpec((tm, d_model), lambda i: (i, 0)),
            pl.BlockSpec((None, C_A + V_WIDTH, d_model), of_layer, pipeline_mode=resident),
            pl.BlockSpec((1, d_model), const),
            pl.BlockSpec((None, d_model, n_cq), of_layer, pipeline_mode=resident),
            pl.BlockSpec((None,) + ckv.shape[1:], of_layer, pipeline_mode=resident),
            pl.BlockSpec((None, n_cq, d_model), of_layer, pipeline_mode=resident),
            pl.BlockSpec((1, d_model), const),
        ],
        out_specs=(pl.BlockSpec((tm, d_model), lambda i: (i, 0)),
                   pl.BlockSpec((tm, d_model), lambda i: (i, 0))),
        compiler_params=_params("arbitrary"),
        name="mix_cross",
    )(a, d, x, w_out, g_cross, w_cq, ckv, w_co, g_ffn)


def _ffn_kernel(hf_ref, halo_ref, x_ref, wg_ref, wv_ref, cwg_ref, cwv_ref, cbg_ref, cbv_ref, wd_ref,
                gout_ref, o_ref, ext_ref, ug_ref, uv_ref, *, tm, norm_output):
    @pl.when(pl.program_id(1) == 0)
    def _():
        halo = halo_ref[...]
        ext_ref[0:HALO_F, :] = jnp.where(pl.program_id(0) > 0, halo, jnp.zeros_like(halo))
        ext_ref[HALO_F:, :] = hf_ref[...]
        o_ref[...] = x_ref[...]

    ext = ext_ref[...]
    ug_ref[...] = _dot(ext, wg_ref[...])
    uv_ref[...] = _dot(ext, wv_ref[...])

    def conv(u_ref, cw_ref, cb_ref, row0, rows):
        out = cb_ref[...]
        for j in range(CONV_F_WIDTH):
            out = out + cw_ref[j:j + 1, :] * u_ref[pl.ds(HALO_F - (CONV_F_WIDTH - 1) + j + row0, rows), :]
        return out

    rows = tm // FFN_ROW_PIECES
    for r in range(FFN_ROW_PIECES):
        y = (jax.nn.silu(conv(ug_ref, cwg_ref, cbg_ref, r * rows, rows))
             * conv(uv_ref, cwv_ref, cbv_ref, r * rows, rows))
        o_ref[r * rows:(r + 1) * rows, :] += _dot(y.astype(BF16), wd_ref[...])

    if norm_output:
        @pl.when(pl.program_id(1) == pl.num_programs(1) - 1)
        def _():
            o_ref[...] = _rms(o_ref[...], gout_ref[...], NORM_EPS)


def _ffn(hf, x, w_up, cw, cb, w_down, g_out, *, layer, tm, tf, norm_output):
    s, d_model = x.shape
    d_ff = w_down.shape[1]
    nf = d_ff // tf
    halo_blocks = tm // HALO_F
    return pl.pallas_call(
        functools.partial(_ffn_kernel, tm=tm, norm_output=norm_output),
        out_shape=jax.ShapeDtypeStruct((s, d_model), F32),
        grid=(s // tm, nf),
        in_specs=[
            pl.BlockSpec((tm, d_model), lambda i, f: (i, 0)),
            pl.BlockSpec((HALO_F, d_model), lambda i, f: (jnp.maximum(i * halo_blocks - 1, 0), 0)),
            pl.BlockSpec((tm, d_model), lambda i, f: (i, 0)),
            pl.BlockSpec((None, d_model, tf), lambda i, f: (layer, 0, f)),
            pl.BlockSpec((None, d_model, tf), lambda i, f: (layer, 0, nf + f)),
            pl.BlockSpec((CONV_F_WIDTH, tf), lambda i, f: (0, f)),
            pl.BlockSpec((CONV_F_WIDTH, tf), lambda i, f: (0, nf + f)),
            pl.BlockSpec((1, tf), lambda i, f: (0, f)),
            pl.BlockSpec((1, tf), lambda i, f: (0, nf + f)),
            pl.BlockSpec((None, tf, d_model), lambda i, f: (layer, f, 0)),
            pl.BlockSpec((1, d_model), lambda i, f: (0, 0)),
        ],
        out_specs=pl.BlockSpec((tm, d_model), lambda i, f: (i, 0)),
        scratch_shapes=[
            pltpu.VMEM((tm + HALO_F, d_model), BF16),
            pltpu.VMEM((tm + HALO_F, tf), F32),
            pltpu.VMEM((tm + HALO_F, tf), F32),
        ],
        compiler_params=_params("arbitrary", "arbitrary"),
        name="conv_ffn",
    )(hf, hf, x, w_up, w_up, cw, cw, cb, cb, w_down, g_out)


def kernel(x, mem, rel_bias_table, g_mix, w_in, conv_a_w, conv_a_b, ln_a_g, ln_a_b, diff_lambda, subln_g, w_out, g_cross, g_mem, w_cq, w_ckv, w_co, g_ffn, w_up, conv_f_w, conv_f_b, w_down, g_final):
    b, s, d_model = x.shape
    assert b == 1 and mem.shape[0] == 1
    depth = w_in.shape[0]
    d_ff = w_down.shape[1]

    tk_attn = _pick_tile(s, 512)
    tq_attn = _pick_tile(s, 1024)
    assert tk_attn >= MAX_DISTANCE, "the two biased key blocks must cover every bucketed distance"
    tm_proj = _pick_tile(s, 1024)
    tn_proj = _pick_tile(w_in.shape[2], 1024)
    ts_conv = _pick_tile(s, 512)
    tm_mix = _pick_tile(s, 512)
    tm_ffn = _pick_tile(s, 512)
    tf_ffn = _pick_tile(d_ff, 512)

    row = lambda p: p.reshape(depth, 1, p.shape[-1])
    w_in_b, w_out_b, w_cq_b = w_in.astype(BF16), w_out.astype(BF16), w_cq.astype(BF16)
    w_ckv_b, w_co_b = w_ckv.astype(BF16), w_co.astype(BF16)
    w_up_b, w_down_b = w_up.astype(BF16), w_down.astype(BF16)
    g_mix, g_cross, g_ffn, g_mem = row(g_mix), row(g_cross), row(g_ffn), row(g_mem)
    conv_a_b, ln_a_g, ln_a_b = row(conv_a_b), row(ln_a_g), row(ln_a_b)
    subln_g, conv_f_b = row(subln_g), row(conv_f_b)

    bias_tiles = _bias_tiles(rel_bias_table, tk_attn)
    ckv = _mem_kv(mem[0], g_mem, w_ckv_b)

    xs = x[0]
    for l in range(depth):
        lambda_init = 0.8 - 0.6 * math.exp(-0.3 * l)
        proj = _in_proj(xs, g_mix[l], w_in_b, layer=l, tm=tm_proj, tn=tn_proj)
        a = _conv_module(proj, conv_a_w[l], conv_a_b[l], ln_a_g[l], ln_a_b[l], ts=ts_conv)
        d = _diff_attn(proj, bias_tiles, diff_lambda[l], subln_g[l], tq=tq_attn, tk=tk_attn,
                       lambda_init=lambda_init)
        xs, hf = _mix_cross(a, d, xs, w_out_b, g_cross[l], w_cq_b, ckv, w_co_b, g_ffn[l],
                            layer=l, tm=tm_mix)
        xs = _ffn(hf, xs, w_up_b, conv_f_w[l], conv_f_b[l], w_down_b, g_final.reshape(1, d_model),
                  layer=l, tm=tm_ffn, tf=tf_ffn, norm_output=(l == depth - 1))
    return xs[None]
```

```python
import functools
import math

import jax
import jax.numpy as jnp
from jax import lax
from jax.experimental import pallas as pl
from jax.experimental.pallas import tpu as pltpu

N_MEM = 256
C_A = 1024
CONV_A_WIDTH = 31
H_DIFF = 8
DH_DIFF = 64
QK_WIDTH = 2 * H_DIFF * DH_DIFF
V_WIDTH = H_DIFF * 2 * DH_DIFF
N_BUCKETS = 32
MAX_DISTANCE = 128
H_CROSS = 4
DH_CROSS = 128
CONV_F_WIDTH = 3
NORM_EPS = 1e-6
SUBLN_EPS = 1e-5
NEG_INF = -1e30
LOG2E = math.log2(math.e)

LANES = 128
SUBLANES = 8
BF16_SUBLANES = 16
VMEM_LIMIT_BYTES = 56 * 1024 * 1024

F32 = jnp.float32
BF16 = jnp.bfloat16

HALO_A = 32
HALO_F = BF16_SUBLANES
VT_CHUNK = 256
Q_CHUNK = 256
CONV_LANES = 512
FFN_ROW_PIECES = 2
PREFETCH_AHEAD = 2


def _params(*sem):
    return pltpu.CompilerParams(dimension_semantics=sem, vmem_limit_bytes=VMEM_LIMIT_BYTES)


def _rms(x, g, eps):
    return x * lax.rsqrt(jnp.mean(x * x, axis=-1, keepdims=True) + eps) * g


def _dot(a, b):
    return jnp.dot(a, b, preferred_element_type=F32)


def _dot_nt(a, b):
    return lax.dot_general(a, b, (((1,), (1,)), ((), ())), preferred_element_type=F32)


def _pick_tile(n, target):
    t = min(n, target)
    while n % t:
        t //= 2
    return t


def _bias_tiles_kernel(table_ref, o_ref, *, t):
    ki = lax.broadcasted_iota(jnp.int32, (t, t), 0)
    qi = lax.broadcasted_iota(jnp.int32, (t, t), 1)
    max_exact = N_BUCKETS // 2
    for j in range(2):
        n = qi - ki + j * t
        nf = jnp.maximum(n, 1).astype(F32)
        large = max_exact + (jnp.log(nf / max_exact) / math.log(MAX_DISTANCE / max_exact)
                             * (N_BUCKETS - max_exact)).astype(jnp.int32)
        large = jnp.minimum(large, N_BUCKETS - 1)
        bucket = jnp.where(n < max_exact, n, large)
        for h in range(H_DIFF):
            far = table_ref[N_BUCKETS - 1, h]
            b = jnp.zeros((t, t), F32)
            for k in range(N_BUCKETS - 1):
                b = jnp.where(bucket == k, (table_ref[k, h] - far) * LOG2E, b)
            if j == 0:
                b = jnp.where(n >= 0, b, NEG_INF)
            o_ref[h, j] = b


def _bias_tiles(table, t):
    return pl.pallas_call(
        functools.partial(_bias_tiles_kernel, t=t),
        out_shape=jax.ShapeDtypeStruct((H_DIFF, 2, t, t), F32),
        in_specs=[pl.BlockSpec(memory_space=pltpu.SMEM)],
        out_specs=pl.BlockSpec(memory_space=pltpu.VMEM),
        compiler_params=pltpu.CompilerParams(vmem_limit_bytes=VMEM_LIMIT_BYTES),
        name="bias_tiles",
    )(table)


def _mem_kv_kernel(mem_ref, g_ref, w_ref, o_ref):
    m = _rms(mem_ref[...], g_ref[...], NORM_EPS).astype(BF16)
    o_ref[...] = _dot(m, w_ref[...]).astype(o_ref.dtype)


def _mem_kv(mem, g_mem, w_ckv):
    depth, d_model, n_out = w_ckv.shape
    n_mem = mem.shape[0]
    return pl.pallas_call(
        _mem_kv_kernel,
        out_shape=jax.ShapeDtypeStruct((depth, n_mem, n_out), BF16),
        grid=(depth,),
        in_specs=[
            pl.BlockSpec((n_mem, d_model), lambda l: (0, 0)),
            pl.BlockSpec((None, 1, d_model), lambda l: (l, 0, 0)),
            pl.BlockSpec((None, d_model, n_out), lambda l: (l, 0, 0)),
        ],
        out_specs=pl.BlockSpec((None, n_mem, n_out), lambda l: (l, 0, 0)),
        compiler_params=_params("arbitrary"),
        name="mem_kv",
    )(mem, g_mem, w_ckv)


def _in_proj_kernel(x_ref, g_ref, w_ref, scale_ref, o_ref, h_ref):
    @pl.when(pl.program_id(1) == 0)
    def _():
        h_ref[...] = _rms(x_ref[...], g_ref[...], NORM_EPS).astype(h_ref.dtype)

    o_ref[...] = (_dot(h_ref[...], w_ref[...]) * scale_ref[...]).astype(o_ref.dtype)


def _in_proj(x, g, w, *, layer, tm, tn):
    s, d_model = x.shape
    n = w.shape[2]
    col = jnp.arange(n)
    is_q = jnp.logical_and(col >= 2 * C_A, col < 2 * C_A + QK_WIDTH)
    scale = jnp.where(is_q, DH_DIFF ** -0.5 * LOG2E, 1.0).astype(F32).reshape(1, n)
    return pl.pallas_call(
        _in_proj_kernel,
        out_shape=jax.ShapeDtypeStruct((s, n), BF16),
        grid=(s // tm, n // tn),
        in_specs=[
            pl.BlockSpec((tm, d_model), lambda i, j: (i, 0)),
            pl.BlockSpec((1, d_model), lambda i, j: (0, 0)),
            pl.BlockSpec((None, d_model, tn), lambda i, j: (layer, 0, j)),
            pl.BlockSpec((1, tn), lambda i, j: (0, j)),
        ],
        out_specs=pl.BlockSpec((tm, tn), lambda i, j: (i, j)),
        scratch_shapes=[pltpu.VMEM((tm, d_model), BF16)],
        compiler_params=_params("arbitrary", "arbitrary"),
        name="in_proj",
    )(x, g, w, scale)


def _conv_module_kernel(main_ref, halo_ref, cw_ref, cb_ref, lg_ref, lb_ref, o_ref,
                        g_ref, sh_ref, conv_ref, *, ts, rows):
    def glu(v):
        v = v.astype(F32)
        return v[:, :C_A] * jax.nn.sigmoid(v[:, C_A:])

    g_ref[0:HALO_A, :] = jnp.where(pl.program_id(0) > 0, glu(halo_ref[...]), 0.0)
    g_ref[HALO_A:, :] = glu(main_ref[...])
    sh_rows = sh_ref.shape[1]
    for k in range(1, SUBLANES):
        sh_ref[k - 1] = g_ref[k:k + sh_rows, :]

    first_tap = HALO_A - (CONV_A_WIDTH - 1)
    groups = rows // SUBLANES

    width = CONV_LANES
    for cols in [slice(lo, lo + width) for lo in range(0, C_A, width)]:
        def chunk(r, carry, cols=cols):
            base = pl.multiple_of(r * rows, rows)
            acc = jnp.broadcast_to(cb_ref[:, cols].reshape(1, 1, width), (groups, SUBLANES, width))
            for j in range(CONV_A_WIDTH):
                k = (first_tap + j) % SUBLANES
                src = g_ref if k == 0 else sh_ref.at[k - 1]
                win = src[pl.ds(base + first_tap + j - k, rows), cols]
                acc = acc + cw_ref[j, :, cols][None] * win.reshape(groups, SUBLANES, width)
            conv_ref[pl.ds(base, rows), cols] = acc.reshape(rows, width)
            return carry

        lax.fori_loop(0, ts // rows, chunk, 0)

    u = conv_ref[...]
    xc = u - jnp.mean(u, axis=-1, keepdims=True)
    y = xc * lax.rsqrt(jnp.mean(xc * xc, axis=-1, keepdims=True) + NORM_EPS)
    y = y * lg_ref[...] + lb_ref[...]
    o_ref[...] = (y * jax.nn.sigmoid(y)).astype(o_ref.dtype)


def _conv_module(proj, cw, cb, lg, lb, *, ts, rows=4 * SUBLANES):
    s = proj.shape[0]
    halo_blocks = ts // HALO_A
    cw = jnp.broadcast_to(cw[:, None, :], (CONV_A_WIDTH, SUBLANES, C_A))
    return pl.pallas_call(
        functools.partial(_conv_module_kernel, ts=ts, rows=rows),
        out_shape=jax.ShapeDtypeStruct((s, C_A), BF16),
        grid=(s // ts,),
        in_specs=[
            pl.BlockSpec((ts, 2 * C_A), lambda i: (i, 0)),
            pl.BlockSpec((HALO_A, 2 * C_A), lambda i: (jnp.maximum(i * halo_blocks - 1, 0), 0)),
            pl.BlockSpec((CONV_A_WIDTH, SUBLANES, C_A), lambda i: (0, 0, 0)),
            pl.BlockSpec((1, C_A), lambda i: (0, 0)),
            pl.BlockSpec((1, C_A), lambda i: (0, 0)),
            pl.BlockSpec((1, C_A), lambda i: (0, 0)),
        ],
        out_specs=pl.BlockSpec((ts, C_A), lambda i: (i, 0)),
        scratch_shapes=[pltpu.VMEM((ts + HALO_A, C_A), F32),
                        pltpu.VMEM((SUBLANES - 1, ts + HALO_A - SUBLANES, C_A), F32),
                        pltpu.VMEM((ts, C_A), F32)],
        compiler_params=_params("arbitrary"),
        name="conv_module",
    )(proj, proj, cw, cb, lg, lb)


def _diff_attn_kernel(q_ref, k_ref, v_ref, bias_ref, lam_ref, g_ref, o_ref,
                      qs_ref, vt_ref, s_ref, m_ref, l_ref, acc_ref, *, tq, tk, lambda_init):
    qi = pl.program_id(1)
    s_len = v_ref.shape[0]
    ratio = tq // tk
    assert ratio % 2 == 0, "the static score double-buffering needs an even step count per tile"
    chunks =[slice(c * Q_CHUNK, (c + 1) * Q_CHUNK) for c in range(2 * tq // Q_CHUNK)]
    sub_tile = [(c * Q_CHUNK % tq) // tk for c in range(len(chunks))]
    sub_cols = [slice(c * Q_CHUNK % tk, c * Q_CHUNK % tk + Q_CHUNK) for c in range(len(chunks))]

    def load_queries(tile):
        q = q_ref[pl.ds(pl.multiple_of(tile * tq, tq), tq), :]
        lane = lax.broadcasted_iota(jnp.int32, q.shape, 1)
        zero = jnp.zeros_like(q)
        qs_ref[0:tq, :] = jnp.where(lane < DH_DIFF, q, zero)
        qs_ref[tq:, :] = jnp.where(lane >= DH_DIFF, q, zero)

    def scores(kb, c):
        return _dot_nt(k_ref[pl.ds(pl.multiple_of(kb * tk, tk), tk), :], qs_ref[chunks[c], :])

    @pl.when(qi == 0)
    def _():
        for c in range(s_len // VT_CHUNK):
            vt_ref[:, c * VT_CHUNK:(c + 1) * VT_CHUNK] = v_ref[c * VT_CHUNK:(c + 1) * VT_CHUNK, :].T
        load_queries(0)
        for c in range(len(chunks)):
            s_ref[0, c] = scores(0, c)

    m_ref[...] = jnp.full(m_ref.shape, NEG_INF, F32)
    l_ref[...] = jnp.zeros(l_ref.shape, F32)
    acc_ref[...] = jnp.zeros(acc_ref.shape, F32)

    def step(buf, kb, rel, next_kb, next_rel):
        vt_blk = vt_ref[:, pl.ds(pl.multiple_of(kb * tk, tk), tk)]
        wanted = [c for c in range(len(chunks)) if next_rel is None or sub_tile[c] - next_rel >= 0]

        def prefetch(i):
            if i < len(wanted):
                s_ref[1 - buf, wanted[i]] = scores(next_kb, wanted[i])

        for i in range(PREFETCH_AHEAD):
            prefetch(i)
        for c, cols in enumerate(chunks):
            prefetch(c + PREFETCH_AHEAD)
            back = None if rel is None else sub_tile[c] - rel
            if back is not None and back < 0:
                continue
            s = s_ref[buf, c]
            if back is not None and back <= 1:
                s = s + bias_ref[back, :, sub_cols[c]]
            m_prev = m_ref[:, cols]
            m_new = jnp.maximum(m_prev, jnp.max(s, axis=0, keepdims=True))
            alpha = jnp.exp2(m_prev - m_new)
            p = jnp.exp2(s - m_new)
            l_ref[:, cols] = alpha * l_ref[:, cols] + jnp.sum(p, axis=0, keepdims=True)
            acc_ref[:, cols] = alpha * acc_ref[:, cols] + _dot(vt_blk, p.astype(BF16))
            m_ref[:, cols] = m_new

    first = ratio * qi

    def far_pair(kb):
        step(0, kb, None, kb + 1, None)
        step(1, kb + 1, None, kb + 2, None)

    def far_quad(i, carry):
        far_pair(4 * i)
        far_pair(4 * i + 2)
        return carry

    pairs = jnp.maximum((first - 2) // 2, 0)
    lax.fori_loop(0, pairs // 2, far_quad, 0)

    @pl.when(pairs % 2 == 1)
    def _():
        far_pair(2 * (pairs - 1))

    @pl.when(qi >= 1)
    def _():
        step(0, first - 2, None, first - 1, None)
        step(1, first - 1, -1, first, 0)

    for rel in range(ratio - 1):
        step(rel % 2, first + rel, rel, first + rel + 1, rel + 1)

    load_queries(jnp.minimum(qi + 1, pl.num_programs(1) - 1))
    step((ratio - 1) % 2, first + ratio - 1, ratio - 1, 0, None)

    lam_p = lam_ref[...]
    lam = (jnp.exp(jnp.sum(lam_p[0:1] * lam_p[1:2], axis=-1, keepdims=True))
           - jnp.exp(jnp.sum(lam_p[2:3] * lam_p[3:4], axis=-1, keepdims=True)) + lambda_init)
    o = acc_ref[...] / l_ref[...]
    d = o[:, 0:tq] - lam * o[:, tq:]
    d = d * lax.rsqrt(jnp.mean(d * d, axis=0, keepdims=True) + SUBLN_EPS)
    o_ref[...] = (d.T * g_ref[...] * (1.0 - lambda_init)).astype(o_ref.dtype)


def _diff_attn(proj, bias_tiles, lam_p, g, *, tq, tk, lambda_init):
    s = proj.shape[0]
    dv = 2 * DH_DIFF
    q_col = 2 * C_A // dv
    k_col = q_col + QK_WIDTH // dv
    v_col = k_col + QK_WIDTH // dv
    return pl.pallas_call(
        functools.partial(_diff_attn_kernel, tq=tq, tk=tk, lambda_init=lambda_init),
        out_shape=jax.ShapeDtypeStruct((s, V_WIDTH), BF16),
        grid=(H_DIFF, s // tq),
        in_specs=[
            pl.BlockSpec((s, dv), lambda h, i: (0, q_col + h)),
            pl.BlockSpec((s, dv), lambda h, i: (0, k_col + h)),
            pl.BlockSpec((s, dv), lambda h, i: (0, v_col + h)),
            pl.BlockSpec((None, 2, tk, tk), lambda h, i: (h, 0, 0, 0)),
            pl.BlockSpec((4, DH_DIFF), lambda h, i: (0, 0)),
            pl.BlockSpec((1, dv), lambda h, i: (0, 0)),
        ],
        out_specs=pl.BlockSpec((tq, dv), lambda h, i: (i, h)),
        scratch_shapes=[
            pltpu.VMEM((2 * tq, dv), BF16),
            pltpu.VMEM((dv, s), BF16),
            pltpu.VMEM((2, 2 * tq // Q_CHUNK, tk, Q_CHUNK), F32),
            pltpu.VMEM((1, 2 * tq), F32),
            pltpu.VMEM((1, 2 * tq), F32),
            pltpu.VMEM((dv, 2 * tq), F32),
        ],
        compiler_params=_params("arbitrary", "arbitrary"),
        name="diff_attn",
    )(proj, proj, proj, bias_tiles, lam_p, g)


def _mix_cross_kernel(a_ref, d_ref, x_ref, wo_ref, gc_ref, wcq_ref, ckv_ref, wco_ref, gf_ref,
                      xo_ref, hf_ref):
    x1 = x_ref[...] + _dot(a_ref[...], wo_ref[0:C_A, :]) + _dot(d_ref[...], wo_ref[C_A:, :])
    hc = _rms(x1, gc_ref[...], NORM_EPS).astype(BF16)
    cq = _dot(hc, wcq_ref[...]).astype(BF16)
    kv_off = H_CROSS * DH_CROSS
    heads = []
    for h in range(H_CROSS):
        lo, hi = h * DH_CROSS, (h + 1) * DH_CROSS
        cl = _dot_nt(cq[:, lo:hi], ckv_ref[:, lo:hi]) * (DH_CROSS ** -0.5)
        e = jnp.exp(cl - jnp.max(cl, axis=-1, keepdims=True))
        l = jnp.sum(e, axis=-1, keepdims=True)
        heads.append((_dot(e.astype(BF16), ckv_ref[:, kv_off + lo:kv_off + hi]) / l).astype(BF16))
    x2 = x1 + _dot(jnp.concatenate(heads, axis=1), wco_ref[...])
    xo_ref[...] = x2
    hf_ref[...] = _rms(x2, gf_ref[...], NORM_EPS).astype(hf_ref.dtype)


def _mix_cross(a, d, x, w_out, g_cross, w_cq, ckv, w_co, g_ffn, *, layer, tm):
    s, d_model = x.shape
    n_cq = w_cq.shape[2]
    const = lambda i: (0, 0)
    of_layer = lambda i: (layer, 0, 0)
    resident = pl.Buffered(1)
    return pl.pallas_call(
        _mix_cross_kernel,
        out_shape=(jax.ShapeDtypeStruct((s, d_model), F32), jax.ShapeDtypeStruct((s, d_model), BF16)),
        grid=(s // tm,),
        in_specs=[
            pl.BlockSpec((tm, C_A), lambda i: (i, 0)),
            pl.BlockSpec((tm, V_WIDTH), lambda i: (i, 0)),
            pl.BlockSpec((tm, d_model), lambda i: (i, 0)),
            pl.BlockSpec((None, C_A + V_WIDTH, d_model), of_layer, pipeline_mode=resident),
            pl.BlockSpec((1, d_model), const),
            pl.BlockSpec((None, d_model, n_cq), of_layer, pipeline_mode=resident),
            pl.BlockSpec((None,) + ckv.shape[1:], of_layer, pipeline_mode=resident),
            pl.BlockSpec((None, n_cq, d_model), of_layer, pipeline_mode=resident),
            pl.BlockSpec((1, d_model), const),
        ],
        out_specs=(pl.BlockSpec((tm, d_model), lambda i: (i, 0)),
                   pl.BlockSpec((tm, d_model), lambda i: (i, 0))),
        compiler_params=_params("arbitrary"),
        name="mix_cross",
    )(a, d, x, w_out, g_cross, w_cq, ckv, w_co, g_ffn)


def _ffn_kernel(hf_ref, halo_ref, x_ref, wg_ref, wv_ref, cwg_ref, cwv_ref, cbg_ref, cbv_ref, wd_ref,
                gout_ref, o_ref, ext_ref, ug_ref, uv_ref, *, tm, norm_output):
    @pl.when(pl.program_id(1) == 0)
    def _():
        halo = halo_ref[...]
        ext_ref[0:HALO_F, :] = jnp.where(pl.program_id(0) > 0, halo, jnp.zeros_like(halo))
        ext_ref[HALO_F:, :] = hf_ref[...]
        o_ref[...] = x_ref[...]

    ext = ext_ref[...]
    ug_ref[...] = _dot(ext, wg_ref[...])
    uv_ref[...] = _dot(ext, wv_ref[...])

    def conv(u_ref, cw_ref, cb_ref, row0, rows):
        out = cb_ref[...]
        for j in range(CONV_F_WIDTH):
            out = out + cw_ref[j:j + 1, :] * u_ref[pl.ds(HALO_F - (CONV_F_WIDTH - 1) + j + row0, rows), :]
        return out

    rows = tm // FFN_ROW_PIECES
    for r in range(FFN_ROW_PIECES):
        y = (jax.nn.silu(conv(ug_ref, cwg_ref, cbg_ref, r * rows, rows))
             * conv(uv_ref, cwv_ref, cbv_ref, r * rows, rows))
        o_ref[r * rows:(r + 1) * rows, :] += _dot(y.astype(BF16), wd_ref[...])

    if norm_output:
        @pl.when(pl.program_id(1) == pl.num_programs(1) - 1)
        def _():
            o_ref[...] = _rms(o_ref[...], gout_ref[...], NORM_EPS)


def _ffn(hf, x, w_up, cw, cb, w_down, g_out, *, layer, tm, tf, norm_output):
    s, d_model = x.shape
    d_ff = w_down.shape[1]
    nf = d_ff // tf
    halo_blocks = tm // HALO_F
    return pl.pallas_call(
        functools.partial(_ffn_kernel, tm=tm, norm_output=norm_output),
        out_shape=jax.ShapeDtypeStruct((s, d_model), F32),
        grid=(s // tm, nf),
        in_specs=[
            pl.BlockSpec((tm, d_model), lambda i, f: (i, 0)),
            pl.BlockSpec((HALO_F, d_model), lambda i, f: (jnp.maximum(i * halo_blocks - 1, 0), 0)),
            pl.BlockSpec((tm, d_model), lambda i, f: (i, 0)),
            pl.BlockSpec((None, d_model, tf), lambda i, f: (layer, 0, f)),
            pl.BlockSpec((None, d_model, tf), lambda i, f: (layer, 0, nf + f)),
            pl.BlockSpec((CONV_F_WIDTH, tf), lambda i, f: (0, f)),
            pl.BlockSpec((CONV_F_WIDTH, tf), lambda i, f: (0, nf + f)),
            pl.BlockSpec((1, tf), lambda i, f: (0, f)),
            pl.BlockSpec((1, tf), lambda i, f: (0, nf + f)),
            pl.BlockSpec((None, tf, d_model), lambda i, f: (layer, f, 0)),
            pl.BlockSpec((1, d_model), lambda i, f: (0, 0)),
        ],
        out_specs=pl.BlockSpec((tm, d_model), lambda i, f: (i, 0)),
        scratch_shapes=[
            pltpu.VMEM((tm + HALO_F, d_model), BF16),
            pltpu.VMEM((tm + HALO_F, tf), F32),
            pltpu.VMEM((tm + HALO_F, tf), F32),
        ],
        compiler_params=_params("arbitrary", "arbitrary"),
        name="conv_ffn",
    )(hf, hf, x, w_up, w_up, cw, cw, cb, cb, w_down, g_out)


def kernel(x, mem, rel_bias_table, g_mix, w_in, conv_a_w, conv_a_b, ln_a_g, ln_a_b, diff_lambda, subln_g, w_out, g_cross, g_mem, w_cq, w_ckv, w_co, g_ffn, w_up, conv_f_w, conv_f_b, w_down, g_final):
    b, s, d_model = x.shape
    assert b == 1 and mem.shape[0] == 1
    depth = w_in.shape[0]
    d_ff = w_down.shape[1]

    tk_attn = _pick_tile(s, 512)
    tq_attn = _pick_tile(s, 1024)
    assert tk_attn >= MAX_DISTANCE, "the two biased key blocks must cover every bucketed distance"
    tm_proj = _pick_tile(s, 1024)
    tn_proj = _pick_tile(w_in.shape[2], 1024)
    ts_conv = _pick_tile(s, 512)
    tm_mix = _pick_tile(s, 512)
    tm_ffn = _pick_tile(s, 512)
    tf_ffn = _pick_tile(d_ff, 512)

    row = lambda p: p.reshape(depth, 1, p.shape[-1])
    w_in_b, w_out_b, w_cq_b = w_in.astype(BF16), w_out.astype(BF16), w_cq.astype(BF16)
    w_ckv_b, w_co_b = w_ckv.astype(BF16), w_co.astype(BF16)
    w_up_b, w_down_b = w_up.astype(BF16), w_down.astype(BF16)
    g_mix, g_cross, g_ffn, g_mem = row(g_mix), row(g_cross), row(g_ffn), row(g_mem)
    conv_a_b, ln_a_g, ln_a_b = row(conv_a_b), row(ln_a_g), row(ln_a_b)
    subln_g, conv_f_b = row(subln_g), row(conv_f_b)

    bias_tiles = _bias_tiles(rel_bias_table, tk_attn)
    ckv = _mem_kv(mem[0], g_mem, w_ckv_b)

    xs = x[0]
    for l in range(depth):
        lambda_init = 0.8 - 0.6 * math.exp(-0.3 * l)
        proj = _in_proj(xs, g_mix[l], w_in_b, layer=l, tm=tm_proj, tn=tn_proj)
        a = _conv_module(proj, conv_a_w[l], conv_a_b[l], ln_a_g[l], ln_a_b[l], ts=ts_conv)
        d = _diff_attn(proj, bias_tiles, diff_lambda[l], subln_g[l], tq=tq_attn, tk=tk_attn,
                       lambda_init=lambda_init)
        xs, hf = _mix_cross(a, d, xs, w_out_b, g_cross[l], w_cq_b, ckv, w_co_b, g_ffn[l],
                            layer=l, tm=tm_mix)
        xs = _ffn(hf, xs, w_up_b, conv_f_w[l], conv_f_b[l], w_down_b, g_final.reshape(1, d_model),
                  layer=l, tm=tm_ffn, tf=tf_ffn, norm_output=(l == depth - 1))
    return xs[None]
```

```python
import functools
import math

import jax
import jax.numpy as jnp
from jax import lax
from jax.experimental import pallas as pl
from jax.experimental.pallas import tpu as pltpu

N_MEM = 256
C_A = 1024
CONV_A_WIDTH = 31
H_DIFF = 8
DH_DIFF = 64
QK_WIDTH = 2 * H_DIFF * DH_DIFF
V_WIDTH = H_DIFF * 2 * DH_DIFF
N_BUCKETS = 32
MAX_DISTANCE = 128
H_CROSS = 4
DH_CROSS = 128
CONV_F_WIDTH = 3
NORM_EPS = 1e-6
SUBLN_EPS = 1e-5
NEG_INF = -1e30
LOG2E = math.log2(math.e)

LANES = 128
SUBLANES = 8
BF16_SUBLANES = 16
VMEM_LIMIT_BYTES = 56 * 1024 * 1024

F32 = jnp.float32
BF16 = jnp.bfloat16

HALO_A = 32
HALO_F = BF16_SUBLANES
VT_CHUNK = 256
Q_CHUNK = 256
CONV_LANES = 512
FFN_ROW_PIECES = 2
PREFETCH_AHEAD = 2


def _params(*sem):
    return pltpu.CompilerParams(dimension_semantics=sem, vmem_limit_bytes=VMEM_LIMIT_BYTES)


def _rms(x, g, eps):
    return x * lax.rsqrt(jnp.mean(x * x, axis=-1, keepdims=True) + eps) * g


def _dot(a, b):
    return jnp.dot(a, b, preferred_element_type=F32)


def _dot_nt(a, b):
    return lax.dot_general(a, b, (((1,), (1,)), ((), ())), preferred_element_type=F32)


def _pick_tile(n, target):
    t = min(n, target)
    while n % t:
        t //= 2
    return t


def _bias_tiles_kernel(table_ref, o_ref, *, t):
    ki = lax.broadcasted_iota(jnp.int32, (t, t), 0)
    qi = lax.broadcasted_iota(jnp.int32, (t, t), 1)
    max_exact = N_BUCKETS // 2
    for j in range(2):
        n = qi - ki + j * t
        nf = jnp.maximum(n, 1).astype(F32)
        large = max_exact + (jnp.log(nf / max_exact) / math.log(MAX_DISTANCE / max_exact)
                             * (N_BUCKETS - max_exact)).astype(jnp.int32)
        large = jnp.minimum(large, N_BUCKETS - 1)
        bucket = jnp.where(n < max_exact, n, large)
        for h in range(H_DIFF):
            far = table_ref[N_BUCKETS - 1, h]
            b = jnp.zeros((t, t), F32)
            for k in range(N_BUCKETS - 1):
                b = jnp.where(bucket == k, (table_ref[k, h] - far) * LOG2E, b)
            if j == 0:
                b = jnp.where(n >= 0, b, NEG_INF)
            o_ref[h, j] = b


def _bias_tiles(table, t):
    return pl.pallas_call(
        functools.partial(_bias_tiles_kernel, t=t),
        out_shape=jax.ShapeDtypeStruct((H_DIFF, 2, t, t), F32),
        in_specs=[pl.BlockSpec(memory_space=pltpu.SMEM)],
        out_specs=pl.BlockSpec(memory_space=pltpu.VMEM),
        compiler_params=pltpu.CompilerParams(vmem_limit_bytes=VMEM_LIMIT_BYTES),
        name="bias_tiles",
    )(table)


def _mem_kv_kernel(mem_ref, g_ref, w_ref, o_ref):
    m = _rms(mem_ref[...], g_ref[...], NORM_EPS).astype(BF16)
    o_ref[...] = _dot(m, w_ref[...]).astype(o_ref.dtype)


def _mem_kv(mem, g_mem, w_ckv):
    depth, d_model, n_out = w_ckv.shape
    n_mem = mem.shape[0]
    return pl.pallas_call(
        _mem_kv_kernel,
        out_shape=jax.ShapeDtypeStruct((depth, n_mem, n_out), BF16),
        grid=(depth,),
        in_specs=[
            pl.BlockSpec((n_mem, d_model), lambda l: (0, 0)),
            pl.BlockSpec((None, 1, d_model), lambda l: (l, 0, 0)),
            pl.BlockSpec((None, d_model, n_out), lambda l: (l, 0, 0)),
        ],
        out_specs=pl.BlockSpec((None, n_mem, n_out), lambda l: (l, 0, 0)),
        compiler_params=_params("arbitrary"),
        name="mem_kv",
    )(mem, g_mem, w_ckv)


def _in_proj_kernel(x_ref, g_ref, w_ref, scale_ref, o_ref, h_ref):
    @pl.when(pl.program_id(1) == 0)
    def _():
        h_ref[...] = _rms(x_ref[...], g_ref[...], NORM_EPS).astype(h_ref.dtype)

    o_ref[...] = (_dot(h_ref[...], w_ref[...]) * scale_ref[...]).astype(o_ref.dtype)


def _in_proj(x, g, w, *, layer, tm, tn):
    s, d_model = x.shape
    n = w.shape[2]
    col = jnp.arange(n)
    is_q = jnp.logical_and(col >= 2 * C_A, col < 2 * C_A + QK_WIDTH)
    scale = jnp.where(is_q, DH_DIFF ** -0.5 * LOG2E, 1.0).astype(F32).reshape(1, n)
    return pl.pallas_call(
        _in_proj_kernel,
        out_shape=jax.ShapeDtypeStruct((s, n), BF16),
        grid=(s // tm, n // tn),
        in_specs=[
            pl.BlockSpec((tm, d_model), lambda i, j: (i, 0)),
            pl.BlockSpec((1, d_model), lambda i, j: (0, 0)),
            pl.BlockSpec((None, d_model, tn), lambda i, j: (layer, 0, j)),
            pl.BlockSpec((1, tn), lambda i, j: (0, j)),
        ],
        out_specs=pl.BlockSpec((tm, tn), lambda i, j: (i, j)),
        scratch_shapes=[pltpu.VMEM((tm, d_model), BF16)],
        compiler_params=_params("arbitrary", "arbitrary"),
        name="in_proj",
    )(x, g, w, scale)


def _conv_module_kernel(main_ref, halo_ref, cw_ref, cb_ref, lg_ref, lb_ref, o_ref,
                        g_ref, sh_ref, conv_ref, *, ts, rows):
    def glu(v):
        v = v.astype(F32)
        return v[:, :C_A] * jax.nn.sigmoid(v[:, C_A:])

    g_ref[0:HALO_A, :] = jnp.where(pl.program_id(0) > 0, glu(halo_ref[...]), 0.0)
    g_ref[HALO_A:, :] = glu(main_ref[...])
    sh_rows = sh_ref.shape[1]
    for k in range(1, SUBLANES):
        sh_ref[k - 1] = g_ref[k:k + sh_rows, :]

    first_tap = HALO_A - (CONV_A_WIDTH - 1)
    groups = rows // SUBLANES

    width = CONV_LANES
    for cols in [slice(lo, lo + width) for lo in range(0, C_A, width)]:
        def chunk(r, carry, cols=cols):
            base = pl.multiple_of(r * rows, rows)
            acc = jnp.broadcast_to(cb_ref[:, cols].reshape(1, 1, width), (groups, SUBLANES, width))
            for j in range(CONV_A_WIDTH):
                k = (first_tap + j) % SUBLANES
                src = g_ref if k == 0 else sh_ref.at[k - 1]
                win = src[pl.ds(base + first_tap + j - k, rows), cols]
                acc = acc + cw_ref[j, :, cols][None] * win.reshape(groups, SUBLANES, width)
            conv_ref[pl.ds(base, rows), cols] = acc.reshape(rows, width)
            return carry

        lax.fori_loop(0, ts // rows, chunk, 0)

    u = conv_ref[...]
    xc = u - jnp.mean(u, axis=-1, keepdims=True)
    y = xc * lax.rsqrt(jnp.mean(xc * xc, axis=-1, keepdims=True) + NORM_EPS)
    y = y * lg_ref[...] + lb_ref[...]
    o_ref[...] = (y * jax.nn.sigmoid(y)).astype(o_ref.dtype)


def _conv_module(proj, cw, cb, lg, lb, *, ts, rows=4 * SUBLANES):
    s = proj.shape[0]
    halo_blocks = ts // HALO_A
    cw = jnp.broadcast_to(cw[:, None, :], (CONV_A_WIDTH, SUBLANES, C_A))
    return pl.pallas_call(
        functools.partial(_conv_module_kernel, ts=ts, rows=rows),
        out_shape=jax.ShapeDtypeStruct((s, C_A), BF16),
        grid=(s // ts,),
        in_specs=[
            pl.BlockSpec((ts, 2 * C_A), lambda i: (i, 0)),
            pl.BlockSpec((HALO_A, 2 * C_A), lambda i: (jnp.maximum(i * halo_blocks - 1, 0), 0)),
            pl.BlockSpec((CONV_A_WIDTH, SUBLANES, C_A), lambda i: (0, 0, 0)),
            pl.BlockSpec((1, C_A), lambda i: (0, 0)),
            pl.BlockSpec((1, C_A), lambda i: (0, 0)),
            pl.BlockSpec((1, C_A), lambda i: (0, 0)),
        ],
        out_specs=pl.BlockSpec((ts, C_A), lambda i: (i, 0)),
        scratch_shapes=[pltpu.VMEM((ts + HALO_A, C_A), F32),
                        pltpu.VMEM((SUBLANES - 1, ts + HALO_A - SUBLANES, C_A), F32),
                        pltpu.VMEM((ts, C_A), F32)],
        compiler_params=_params("arbitrary"),
        name="conv_module",
    )(proj, proj, cw, cb, lg, lb)


def _diff_attn_kernel(q_ref, k_ref, v_ref, bias_ref, lam_ref, g_ref, o_ref,
                      qs_ref, vt_ref, s_ref, m_ref, l_ref, acc_ref, *, tq, tk, lambda_init):
    qi = pl.program_id(1)
    s_len = v_ref.shape[0]
    ratio = tq // tk
    assert ratio % 2 == 0, "the static score double-buffering needs an even step count per tile"
    chunks =[slice(c * Q_CHUNK, (c + 1) * Q_CHUNK) for c in range(2 * tq // Q_CHUNK)]
    sub_tile = [(c * Q_CHUNK % tq) // tk for c in range(len(chunks))]
    sub_cols = [slice(c * Q_CHUNK % tk, c * Q_CHUNK % tk + Q_CHUNK) for c in range(len(chunks))]

    def load_queries(tile):
        q = q_ref[pl.ds(pl.multiple_of(tile * tq, tq), tq), :]
        lane = lax.broadcasted_iota(jnp.int32, q.shape, 1)
        zero = jnp.zeros_like(q)
        qs_ref[0:tq, :] = jnp.where(lane < DH_DIFF, q, zero)
        qs_ref[tq:, :] = jnp.where(lane >= DH_DIFF, q, zero)

    def scores(kb, c):
        return _dot_nt(k_ref[pl.ds(pl.multiple_of(kb * tk, tk), tk), :], qs_ref[chunks[c], :])

    @pl.when(qi == 0)
    def _():
        for c in range(s_len // VT_CHUNK):
            vt_ref[:, c * VT_CHUNK:(c + 1) * VT_CHUNK] = v_ref[c * VT_CHUNK:(c + 1) * VT_CHUNK, :].T
        load_queries(0)
        for c in range(len(chunks)):
            s_ref[0, c] = scores(0, c)

    m_ref[...] = jnp.full(m_ref.shape, NEG_INF, F32)
    l_ref[...] = jnp.zeros(l_ref.shape, F32)
    acc_ref[...] = jnp.zeros(acc_ref.shape, F32)

    def step(buf, kb, rel, next_kb, next_rel):
        vt_blk = vt_ref[:, pl.ds(pl.multiple_of(kb * tk, tk), tk)]
        wanted = [c for c in range(len(chunks)) if next_rel is None or sub_tile[c] - next_rel >= 0]

        def prefetch(i):
            if i < len(wanted):
                s_ref[1 - buf, wanted[i]] = scores(next_kb, wanted[i])

        for i in range(PREFETCH_AHEAD):
            prefetch(i)
        for c, cols in enumerate(chunks):
            prefetch(c + PREFETCH_AHEAD)
            back = None if rel is None else sub_tile[c] - rel
            if back is not None and back < 0:
                continue
            s = s_ref[buf, c]
            if back is not None and back <= 1:
                s = s + bias_ref[back, :, sub_cols[c]]
            m_prev = m_ref[:, cols]
            m_new = jnp.maximum(m_prev, jnp.max(s, axis=0, keepdims=True))
            alpha = jnp.exp2(m_prev - m_new)
            p = jnp.exp2(s - m_new)
            l_ref[:, cols] = alpha * l_ref[:, cols] + jnp.sum(p, axis=0, keepdims=True)
            acc_ref[:, cols] = alpha * acc_ref[:, cols] + _dot(vt_blk, p.astype(BF16))
            m_ref[:, cols] = m_new

    first = ratio * qi

    def far_pair(kb):
        step(0, kb, None, kb + 1, None)
        step(1, kb + 1, None, kb + 2, None)

    def far_quad(i, carry):
        far_pair(4 * i)
        far_pair(4 * i + 2)
        return carry

    pairs = jnp.maximum((first - 2) // 2, 0)
    lax.fori_loop(0, pairs // 2, far_quad, 0)

    @pl.when(pairs % 2 == 1)
    def _():
        far_pair(2 * (pairs - 1))

    @pl.when(qi >= 1)
    def _():
        step(0, first - 2, None, first - 1, None)
        step(1, first - 1, -1, first, 0)

    for rel in range(ratio - 1):
        step(rel % 2, first + rel, rel, first + rel + 1, rel + 1)

    load_queries(jnp.minimum(qi + 1, pl.num_programs(1) - 1))
    step((ratio - 1) % 2, first + ratio - 1, ratio - 1, 0, None)

    lam_p = lam_ref[...]
    lam = (jnp.exp(jnp.sum(lam_p[0:1] * lam_p[1:2], axis=-1, keepdims=True))
           - jnp.exp(jnp.sum(lam_p[2:3] * lam_p[3:4], axis=-1, keepdims=True)) + lambda_init)
    o = acc_ref[...] / l_ref[...]
    d = o[:, 0:tq] - lam * o[:, tq:]
    d = d * lax.rsqrt(jnp.mean(d * d, axis=0, keepdims=True) + SUBLN_EPS)
    o_ref[...] = (d.T * g_ref[...] * (1.0 - lambda_init)).astype(o_ref.dtype)


def _diff_attn(proj, bias_tiles, lam_p, g, *, tq, tk, lambda_init):
    s = proj.shape[0]
    dv = 2 * DH_DIFF
    q_col = 2 * C_A // dv
    k_col = q_col + QK_WIDTH // dv
    v_col = k_col + QK_WIDTH // dv
    return pl.pallas_call(
        functools.partial(_diff_attn_kernel, tq=tq, tk=tk, lambda_init=lambda_init),
        out_shape=jax.ShapeDtypeStruct((s, V_WIDTH), BF16),
        grid=(H_DIFF, s // tq),
        in_specs=[
            pl.BlockSpec((s, dv), lambda h, i: (0, q_col + h)),
            pl.BlockSpec((s, dv), lambda h, i: (0, k_col + h)),
            pl.BlockSpec((s, dv), lambda h, i: (0, v_col + h)),
            pl.BlockSpec((None, 2, tk, tk), lambda h, i: (h, 0, 0, 0)),
            pl.BlockSpec((4, DH_DIFF), lambda h, i: (0, 0)),
            pl.BlockSpec((1, dv), lambda h, i: (0, 0)),
        ],
        out_specs=pl.BlockSpec((tq, dv), lambda h, i: (i, h)),
        scratch_shapes=[
            pltpu.VMEM((2 * tq, dv), BF16),
            pltpu.VMEM((dv, s), BF16),
            pltpu.VMEM((2, 2 * tq // Q_CHUNK, tk, Q_CHUNK), F32),
            pltpu.VMEM((1, 2 * tq), F32),
            pltpu.VMEM((1, 2 * tq), F32),
            pltpu.VMEM((dv, 2 * tq), F32),
        ],
        compiler_params=_params("arbitrary", "arbitrary"),
        name="diff_attn",
    )(proj, proj, proj, bias_tiles, lam_p, g)


def _mix_cross_kernel(a_ref, d_ref, x_ref, wo_ref, gc_ref, wcq_ref, ckv_ref, wco_ref, gf_ref,
                      xo_ref, hf_ref):
    x1 = x_ref[...] + _dot(a_ref[...], wo_ref[0:C_A, :]) + _dot(d_ref[...], wo_ref[C_A:, :])
    hc = _rms(x1, gc_ref[...], NORM_EPS).astype(BF16)
    cq = _dot(hc, wcq_ref[...]).astype(BF16)
    kv_off = H_CROSS * DH_CROSS
    heads = []
    for h in range(H_CROSS):
        lo, hi = h * DH_CROSS, (h + 1) * DH_CROSS
        cl = _dot_nt(cq[:, lo:hi], ckv_ref[:, lo:hi]) * (DH_CROSS ** -0.5)
        e = jnp.exp(cl - jnp.max(cl, axis=-1, keepdims=True))
        l = jnp.sum(e, axis=-1, keepdims=True)
        heads.append((_dot(e.astype(BF16), ckv_ref[:, kv_off + lo:kv_off + hi]) / l).astype(BF16))
    x2 = x1 + _dot(jnp.concatenate(heads, axis=1), wco_ref[...])
    xo_ref[...] = x2
    hf_ref[...] = _rms(x2, gf_ref[...], NORM_EPS).astype(hf_ref.dtype)


def _mix_cross(a, d, x, w_out, g_cross, w_cq, ckv, w_co, g_ffn, *, layer, tm):
    s, d_model = x.shape
    n_cq = w_cq.shape[2]
    const = lambda i: (0, 0)
    of_layer = lambda i: (layer, 0, 0)
    resident = pl.Buffered(1)
    return pl.pallas_call(
        _mix_cross_kernel,
        out_shape=(jax.ShapeDtypeStruct((s, d_model), F32), jax.ShapeDtypeStruct((s, d_model), BF16)),
        grid=(s // tm,),
        in_specs=[
            pl.BlockSpec((tm, C_A), lambda i: (i, 0)),
            pl.BlockSpec((tm, V_WIDTH), lambda i: (i, 0)),
            pl.BlockSpec((tm, d_model), lambda i: (i, 0)),
            pl.BlockSpec((None, C_A + V_WIDTH, d_model), of_layer, pipeline_mode=resident),
            pl.BlockSpec((1, d_model), const),
            pl.BlockSpec((None, d_model, n_cq), of_layer, pipeline_mode=resident),
            pl.BlockSpec((None,) + ckv.shape[1:], of_layer, pipeline_mode=resident),
            pl.BlockSpec((None, n_cq, d_model), of_layer, pipeline_mode=resident),
            pl.BlockSpec((1, d_model), const),
        ],
        out_specs=(pl.BlockSpec((tm, d_model), lambda i: (i, 0)),
                   pl.BlockSpec((tm, d_model), lambda i: (i, 0))),
        compiler_params=_params("arbitrary"),
        name="mix_cross",
    )(a, d, x, w_out, g_cross, w_cq, ckv, w_co, g_ffn)


def _ffn_kernel(hf_ref, halo_ref, x_ref, wg_ref, wv_ref, cwg_ref, cwv_ref, cbg_ref, cbv_ref, wd_ref,
                gout_ref, o_ref, ext_ref, ug_ref, uv_ref, *, tm, norm_output):
    @pl.when(pl.program_id(1) == 0)
    def _():
        halo = halo_ref[...]
        ext_ref[0:HALO_F, :] = jnp.where(pl.program_id(0) > 0, halo, jnp.zeros_like(halo))
        ext_ref[HALO_F:, :] = hf_ref[...]
        o_ref[...] = x_ref[...]

    ext = ext_ref[...]
    ug_ref[...] = _dot(ext, wg_ref[...])
    uv_ref[...] = _dot(ext, wv_ref[...])

    def conv(u_ref, cw_ref, cb_ref, row0, rows):
        out = cb_ref[...]
        for j in range(CONV_F_WIDTH):
            out = out + cw_ref[j:j + 1, :] * u_ref[pl.ds(HALO_F - (CONV_F_WIDTH - 1) + j + row0, rows), :]
        return out

    rows = tm // FFN_ROW_PIECES
    for r in range(FFN_ROW_PIECES):
        y = (jax.nn.silu(conv(ug_ref, cwg_ref, cbg_ref, r * rows, rows))
             * conv(uv_ref, cwv_ref, cbv_ref, r * rows, rows))
        o_ref[r * rows:(r + 1) * rows, :] += _dot(y.astype(BF16), wd_ref[...])

    if norm_output:
        @pl.when(pl.program_id(1) == pl.num_programs(1) - 1)
        def _():
            o_ref[...] = _rms(o_ref[...], gout_ref[...], NORM_EPS)


def _ffn(hf, x, w_up, cw, cb, w_down, g_out, *, layer, tm, tf, norm_output):
    s, d_model = x.shape
    d_ff = w_down.shape[1]
    nf = d_ff // tf
    halo_blocks = tm // HALO_F
    return pl.pallas_call(
        functools.partial(_ffn_kernel, tm=tm, norm_output=norm_output),
        out_shape=jax.ShapeDtypeStruct((s, d_model), F32),
        grid=(s // tm, nf),
        in_specs=[
            pl.BlockSpec((tm, d_model), lambda i, f: (i, 0), pipeline_mode=pl.Buffered(1)),
            pl.BlockSpec((HALO_F, d_model), lambda i, f: (jnp.maximum(i * halo_blocks - 1, 0), 0)),
            pl.BlockSpec((tm, d_model), lambda i, f: (i, 0), pipeline_mode=pl.Buffered(1)),
            pl.BlockSpec((None, d_model, tf), lambda i, f: (layer, 0, f)),
            pl.BlockSpec((None, d_model, tf), lambda i, f: (layer, 0, nf + f)),
            pl.BlockSpec((CONV_F_WIDTH, tf), lambda i, f: (0, f)),
            pl.BlockSpec((CONV_F_WIDTH, tf), lambda i, f: (0, nf + f)),
            pl.BlockSpec((1, tf), lambda i, f: (0, f)),
            pl.BlockSpec((1, tf), lambda i, f: (0, nf + f)),
            pl.BlockSpec((None, tf, d_model), lambda i, f: (layer, f, 0)),
            pl.BlockSpec((1, d_model), lambda i, f: (0, 0)),
        ],
        out_specs=pl.BlockSpec((tm, d_model), lambda i, f: (i, 0)),
        scratch_shapes=[
            pltpu.VMEM((tm + HALO_F, d_model), BF16),
            pltpu.VMEM((tm + HALO_F, tf), F32),
            pltpu.VMEM((tm + HALO_F, tf), F32),
        ],
        compiler_params=_params("arbitrary", "arbitrary"),
        name="conv_ffn",
    )(hf, hf, x, w_up, w_up, cw, cw, cb, cb, w_down, g_out)


def kernel(x, mem, rel_bias_table, g_mix, w_in, conv_a_w, conv_a_b, ln_a_g, ln_a_b, diff_lambda, subln_g, w_out, g_cross, g_mem, w_cq, w_ckv, w_co, g_ffn, w_up, conv_f_w, conv_f_b, w_down, g_final):
    b, s, d_model = x.shape
    assert b == 1 and mem.shape[0] == 1
    depth = w_in.shape[0]
    d_ff = w_down.shape[1]

    tk_attn = _pick_tile(s, 512)
    tq_attn = _pick_tile(s, 1024)
    assert tk_attn >= MAX_DISTANCE, "the two biased key blocks must cover every bucketed distance"
    tm_proj = _pick_tile(s, 1024)
    tn_proj = _pick_tile(w_in.shape[2], 1024)
    ts_conv = _pick_tile(s, 512)
    tm_mix = _pick_tile(s, 512)
    tm_ffn = _pick_tile(s, 1024)
    tf_ffn = _pick_tile(d_ff, 512)

    row = lambda p: p.reshape(depth, 1, p.shape[-1])
    w_in_b, w_out_b, w_cq_b = w_in.astype(BF16), w_out.astype(BF16), w_cq.astype(BF16)
    w_ckv_b, w_co_b = w_ckv.astype(BF16), w_co.astype(BF16)
    w_up_b, w_down_b = w_up.astype(BF16), w_down.astype(BF16)
    g_mix, g_cross, g_ffn, g_mem = row(g_mix), row(g_cross), row(g_ffn), row(g_mem)
    conv_a_b, ln_a_g, ln_a_b = row(conv_a_b), row(ln_a_g), row(ln_a_b)
    subln_g, conv_f_b = row(subln_g), row(conv_f_b)

    bias_tiles = _bias_tiles(rel_bias_table, tk_attn)
    ckv = _mem_kv(mem[0], g_mem, w_ckv_b)

    xs = x[0]
    for l in range(depth):
        lambda_init = 0.8 - 0.6 * math.exp(-0.3 * l)
        proj = _in_proj(xs, g_mix[l], w_in_b, layer=l, tm=tm_proj, tn=tn_proj)
        a = _conv_module(proj, conv_a_w[l], conv_a_b[l], ln_a_g[l], ln_a_b[l], ts=ts_conv)
        d = _diff_attn(proj, bias_tiles, diff_lambda[l], subln_g[l], tq=tq_attn, tk=tk_attn,
                       lambda_init=lambda_init)
        xs, hf = _mix_cross(a, d, xs, w_out_b, g_cross[l], w_cq_b, ckv, w_co_b, g_ffn[l],
                            layer=l, tm=tm_mix)
        xs = _ffn(hf, xs, w_up_b, conv_f_w[l], conv_f_b[l], w_down_b, g_final.reshape(1, d_model),
                  layer=l, tm=tm_ffn, tf=tf_ffn, norm_output=(l == depth - 1))
    return xs[None]
```

```python
import functools
import math

import jax
import jax.numpy as jnp
from jax import lax
from jax.experimental import pallas as pl
from jax.experimental.pallas import tpu as pltpu

N_MEM = 256
C_A = 1024
CONV_A_WIDTH = 31
H_DIFF = 8
DH_DIFF = 64
QK_WIDTH = 2 * H_DIFF * DH_DIFF
V_WIDTH = H_DIFF * 2 * DH_DIFF
N_BUCKETS = 32
MAX_DISTANCE = 128
H_CROSS = 4
DH_CROSS = 128
CONV_F_WIDTH = 3
NORM_EPS = 1e-6
SUBLN_EPS = 1e-5
NEG_INF = -1e30
LOG2E = math.log2(math.e)

LANES = 128
SUBLANES = 8
BF16_SUBLANES = 16
VMEM_LIMIT_BYTES = 56 * 1024 * 1024

F32 = jnp.float32
BF16 = jnp.bfloat16

HALO_A = 32
HALO_F = BF16_SUBLANES
VT_CHUNK = 256
Q_CHUNK = 256
CONV_LANES = 512
FFN_ROW_PIECES = 2
PREFETCH_AHEAD = 2


def _params(*sem):
    return pltpu.CompilerParams(dimension_semantics=sem, vmem_limit_bytes=VMEM_LIMIT_BYTES)


def _rms(x, g, eps):
    return x * lax.rsqrt(jnp.mean(x * x, axis=-1, keepdims=True) + eps) * g


def _dot(a, b):
    return jnp.dot(a, b, preferred_element_type=F32)


def _dot_nt(a, b):
    return lax.dot_general(a, b, (((1,), (1,)), ((), ())), preferred_element_type=F32)


def _pick_tile(n, target):
    t = min(n, target)
    while n % t:
        t //= 2
    return t


def _bias_tiles_kernel(table_ref, o_ref, *, t):
    ki = lax.broadcasted_iota(jnp.int32, (t, t), 0)
    qi = lax.broadcasted_iota(jnp.int32, (t, t), 1)
    max_exact = N_BUCKETS // 2
    for j in range(2):
        n = qi - ki + j * t
        nf = jnp.maximum(n, 1).astype(F32)
        large = max_exact + (jnp.log(nf / max_exact) / math.log(MAX_DISTANCE / max_exact)
                             * (N_BUCKETS - max_exact)).astype(jnp.int32)
        large = jnp.minimum(large, N_BUCKETS - 1)
        bucket = jnp.where(n < max_exact, n, large)
        for h in range(H_DIFF):
            far = table_ref[N_BUCKETS - 1, h]
            b = jnp.zeros((t, t), F32)
            for k in range(N_BUCKETS - 1):
                b = jnp.where(bucket == k, (table_ref[k, h] - far) * LOG2E, b)
            if j == 0:
                b = jnp.where(n >= 0, b, NEG_INF)
            o_ref[h, j] = b


def _bias_tiles(table, t):
    return pl.pallas_call(
        functools.partial(_bias_tiles_kernel, t=t),
        out_shape=jax.ShapeDtypeStruct((H_DIFF, 2, t, t), F32),
        in_specs=[pl.BlockSpec(memory_space=pltpu.SMEM)],
        out_specs=pl.BlockSpec(memory_space=pltpu.VMEM),
        compiler_params=pltpu.CompilerParams(vmem_limit_bytes=VMEM_LIMIT_BYTES),
        name="bias_tiles",
    )(table)


def _mem_kv_kernel(mem_ref, g_ref, w_ref, o_ref):
    m = _rms(mem_ref[...], g_ref[...], NORM_EPS).astype(BF16)
    o_ref[...] = _dot(m, w_ref[...]).astype(o_ref.dtype)


def _mem_kv(mem, g_mem, w_ckv):
    depth, d_model, n_out = w_ckv.shape
    n_mem = mem.shape[0]
    return pl.pallas_call(
        _mem_kv_kernel,
        out_shape=jax.ShapeDtypeStruct((depth, n_mem, n_out), BF16),
        grid=(depth,),
        in_specs=[
            pl.BlockSpec((n_mem, d_model), lambda l: (0, 0)),
            pl.BlockSpec((None, 1, d_model), lambda l: (l, 0, 0)),
            pl.BlockSpec((None, d_model, n_out), lambda l: (l, 0, 0)),
        ],
        out_specs=pl.BlockSpec((None, n_mem, n_out), lambda l: (l, 0, 0)),
        compiler_params=_params("arbitrary"),
        name="mem_kv",
    )(mem, g_mem, w_ckv)


def _in_proj_kernel(x_ref, g_ref, w_ref, scale_ref, o_ref, h_ref):
    @pl.when(pl.program_id(1) == 0)
    def _():
        h_ref[...] = _rms(x_ref[...], g_ref[...], NORM_EPS).astype(h_ref.dtype)

    o_ref[...] = (_dot(h_ref[...], w_ref[...]) * scale_ref[...]).astype(o_ref.dtype)


def _in_proj(x, g, w, *, layer, tm, tn):
    s, d_model = x.shape
    n = w.shape[2]
    col = jnp.arange(n)
    is_q = jnp.logical_and(col >= 2 * C_A, col < 2 * C_A + QK_WIDTH)
    scale = jnp.where(is_q, DH_DIFF ** -0.5 * LOG2E, 1.0).astype(F32).reshape(1, n)
    return pl.pallas_call(
        _in_proj_kernel,
        out_shape=jax.ShapeDtypeStruct((s, n), BF16),
        grid=(s // tm, n // tn),
        in_specs=[
            pl.BlockSpec((tm, d_model), lambda i, j: (i, 0)),
            pl.BlockSpec((1, d_model), lambda i, j: (0, 0)),
            pl.BlockSpec((None, d_model, tn), lambda i, j: (layer, 0, j)),
            pl.BlockSpec((1, tn), lambda i, j: (0, j)),
        ],
        out_specs=pl.BlockSpec((tm, tn), lambda i, j: (i, j)),
        scratch_shapes=[pltpu.VMEM((tm, d_model), BF16)],
        compiler_params=_params("arbitrary", "arbitrary"),
        name="in_proj",
    )(x, g, w, scale)


def _conv_module_kernel(main_ref, halo_ref, cw_ref, cb_ref, lg_ref, lb_ref, o_ref,
                        g_ref, sh_ref, conv_ref, *, ts, rows):
    def glu(v):
        v = v.astype(F32)
        return v[:, :C_A] * jax.nn.sigmoid(v[:, C_A:])

    g_ref[0:HALO_A, :] = jnp.where(pl.program_id(0) > 0, glu(halo_ref[...]), 0.0)
    g_ref[HALO_A:, :] = glu(main_ref[...])
    sh_rows = sh_ref.shape[1]
    for k in range(1, SUBLANES):
        sh_ref[k - 1] = g_ref[k:k + sh_rows, :]

    first_tap = HALO_A - (CONV_A_WIDTH - 1)
    groups = rows // SUBLANES

    width = CONV_LANES
    for cols in [slice(lo, lo + width) for lo in range(0, C_A, width)]:
        def chunk(r, carry, cols=cols):
            base = pl.multiple_of(r * rows, rows)
            acc = jnp.broadcast_to(cb_ref[:, cols].reshape(1, 1, width), (groups, SUBLANES, width))
            for j in range(CONV_A_WIDTH):
                k = (first_tap + j) % SUBLANES
                src = g_ref if k == 0 else sh_ref.at[k - 1]
                win = src[pl.ds(base + first_tap + j - k, rows), cols]
                acc = acc + cw_ref[j, :, cols][None] * win.reshape(groups, SUBLANES, width)
            conv_ref[pl.ds(base, rows), cols] = acc.reshape(rows, width)
            return carry

        lax.fori_loop(0, ts // rows, chunk, 0)

    u = conv_ref[...]
    xc = u - jnp.mean(u, axis=-1, keepdims=True)
    y = xc * lax.rsqrt(jnp.mean(xc * xc, axis=-1, keepdims=True) + NORM_EPS)
    y = y * lg_ref[...] + lb_ref[...]
    o_ref[...] = (y * jax.nn.sigmoid(y)).astype(o_ref.dtype)


def _conv_module(proj, cw, cb, lg, lb, *, ts, rows=4 * SUBLANES):
    s = proj.shape[0]
    halo_blocks = ts // HALO_A
    cw = jnp.broadcast_to(cw[:, None, :], (CONV_A_WIDTH, SUBLANES, C_A))
    return pl.pallas_call(
        functools.partial(_conv_module_kernel, ts=ts, rows=rows),
        out_shape=jax.ShapeDtypeStruct((s, C_A), BF16),
        grid=(s // ts,),
        in_specs=[
            pl.BlockSpec((ts, 2 * C_A), lambda i: (i, 0)),
            pl.BlockSpec((HALO_A, 2 * C_A), lambda i: (jnp.maximum(i * halo_blocks - 1, 0), 0)),
            pl.BlockSpec((CONV_A_WIDTH, SUBLANES, C_A), lambda i: (0, 0, 0)),
            pl.BlockSpec((1, C_A), lambda i: (0, 0)),
            pl.BlockSpec((1, C_A), lambda i: (0, 0)),
            pl.BlockSpec((1, C_A), lambda i: (0, 0)),
        ],
        out_specs=pl.BlockSpec((ts, C_A), lambda i: (i, 0)),
        scratch_shapes=[pltpu.VMEM((ts + HALO_A, C_A), F32),
                        pltpu.VMEM((SUBLANES - 1, ts + HALO_A - SUBLANES, C_A), F32),
                        pltpu.VMEM((ts, C_A), F32)],
        compiler_params=_params("arbitrary"),
        name="conv_module",
    )(proj, proj, cw, cb, lg, lb)


def _diff_attn_kernel(q_ref, k_ref, v_ref, bias_ref, lam_ref, g_ref, o_ref,
                      qs_ref, vt_ref, s_ref, m_ref, l_ref, acc_ref, *, tq, tk, lambda_init):
    qi = pl.program_id(1)
    s_len = v_ref.shape[0]
    ratio = tq // tk
    assert ratio % 2 == 0, "the static score double-buffering needs an even step count per tile"
    chunks =[slice(c * Q_CHUNK, (c + 1) * Q_CHUNK) for c in range(2 * tq // Q_CHUNK)]
    sub_tile = [(c * Q_CHUNK % tq) // tk for c in range(len(chunks))]
    sub_cols = [slice(c * Q_CHUNK % tk, c * Q_CHUNK % tk + Q_CHUNK) for c in range(len(chunks))]

    def load_queries(tile):
        q = q_ref[pl.ds(pl.multiple_of(tile * tq, tq), tq), :]
        lane = lax.broadcasted_iota(jnp.int32, q.shape, 1)
        zero = jnp.zeros_like(q)
        qs_ref[0:tq, :] = jnp.where(lane < DH_DIFF, q, zero)
        qs_ref[tq:, :] = jnp.where(lane >= DH_DIFF, q, zero)

    def scores(kb, c):
        return _dot_nt(k_ref[pl.ds(pl.multiple_of(kb * tk, tk), tk), :], qs_ref[chunks[c], :])

    @pl.when(qi == 0)
    def _():
        for c in range(s_len // VT_CHUNK):
            vt_ref[:, c * VT_CHUNK:(c + 1) * VT_CHUNK] = v_ref[c * VT_CHUNK:(c + 1) * VT_CHUNK, :].T
        load_queries(0)
        for c in range(len(chunks)):
            s_ref[0, c] = scores(0, c)

    m_ref[...] = jnp.full(m_ref.shape, NEG_INF, F32)
    l_ref[...] = jnp.zeros(l_ref.shape, F32)
    acc_ref[...] = jnp.zeros(acc_ref.shape, F32)

    def step(buf, kb, rel, next_kb, next_rel):
        vt_blk = vt_ref[:, pl.ds(pl.multiple_of(kb * tk, tk), tk)]
        wanted = [c for c in range(len(chunks)) if next_rel is None or sub_tile[c] - next_rel >= 0]

        def prefetch(i):
            if i < len(wanted):
                s_ref[1 - buf, wanted[i]] = scores(next_kb, wanted[i])

        for i in range(PREFETCH_AHEAD):
            prefetch(i)
        for c, cols in enumerate(chunks):
            prefetch(c + PREFETCH_AHEAD)
            back = None if rel is None else sub_tile[c] - rel
            if back is not None and back < 0:
                continue
            s = s_ref[buf, c]
            if back is not None and back <= 1:
                s = s + bias_ref[back, :, sub_cols[c]]
            m_prev = m_ref[:, cols]
            m_new = jnp.maximum(m_prev, jnp.max(s, axis=0, keepdims=True))
            alpha = jnp.exp2(m_prev - m_new)
            p = jnp.exp2(s - m_new)
            l_ref[:, cols] = alpha * l_ref[:, cols] + jnp.sum(p, axis=0, keepdims=True)
            acc_ref[:, cols] = alpha * acc_ref[:, cols] + _dot(vt_blk, p.astype(BF16))
            m_ref[:, cols] = m_new

    first = ratio * qi

    def far_pair(kb):
        step(0, kb, None, kb + 1, None)
        step(1, kb + 1, None, kb + 2, None)

    def far_oct(i, carry):
        for k in range(4):
            far_pair(8 * i + 2 * k)
        return carry

    pairs = jnp.maximum((first - 2) // 2, 0)
    lax.fori_loop(0, pairs // 4, far_oct, 0)

    @pl.when(pairs % 4 >= 2)
    def _():
        far_pair(8 * (pairs // 4))
        far_pair(8 * (pairs // 4) + 2)

    @pl.when(pairs % 2 == 1)
    def _():
        far_pair(2 * (pairs - 1))

    @pl.when(qi >= 1)
    def _():
        step(0, first - 2, None, first - 1, None)
        step(1, first - 1, -1, first, 0)

    for rel in range(ratio - 1):
        step(rel % 2, first + rel, rel, first + rel + 1, rel + 1)

    load_queries(jnp.minimum(qi + 1, pl.num_programs(1) - 1))
    step((ratio - 1) % 2, first + ratio - 1, ratio - 1, 0, None)

    lam_p = lam_ref[...]
    lam = (jnp.exp(jnp.sum(lam_p[0:1] * lam_p[1:2], axis=-1, keepdims=True))
           - jnp.exp(jnp.sum(lam_p[2:3] * lam_p[3:4], axis=-1, keepdims=True)) + lambda_init)
    o = acc_ref[...] / l_ref[...]
    d = o[:, 0:tq] - lam * o[:, tq:]
    d = d * lax.rsqrt(jnp.mean(d * d, axis=0, keepdims=True) + SUBLN_EPS)
    o_ref[...] = (d.T * g_ref[...] * (1.0 - lambda_init)).astype(o_ref.dtype)


def _diff_attn(proj, bias_tiles, lam_p, g, *, tq, tk, lambda_init):
    s = proj.shape[0]
    dv = 2 * DH_DIFF
    q_col = 2 * C_A // dv
    k_col = q_col + QK_WIDTH // dv
    v_col = k_col + QK_WIDTH // dv
    return pl.pallas_call(
        functools.partial(_diff_attn_kernel, tq=tq, tk=tk, lambda_init=lambda_init),
        out_shape=jax.ShapeDtypeStruct((s, V_WIDTH), BF16),
        grid=(H_DIFF, s // tq),
        in_specs=[
            pl.BlockSpec((s, dv), lambda h, i: (0, q_col + h)),
            pl.BlockSpec((s, dv), lambda h, i: (0, k_col + h)),
            pl.BlockSpec((s, dv), lambda h, i: (0, v_col + h)),
            pl.BlockSpec((None, 2, tk, tk), lambda h, i: (h, 0, 0, 0)),
            pl.BlockSpec((4, DH_DIFF), lambda h, i: (0, 0)),
            pl.BlockSpec((1, dv), lambda h, i: (0, 0)),
        ],
        out_specs=pl.BlockSpec((tq, dv), lambda h, i: (i, h)),
        scratch_shapes=[
            pltpu.VMEM((2 * tq, dv), BF16),
            pltpu.VMEM((dv, s), BF16),
            pltpu.VMEM((2, 2 * tq // Q_CHUNK, tk, Q_CHUNK), F32),
            pltpu.VMEM((1, 2 * tq), F32),
            pltpu.VMEM((1, 2 * tq), F32),
            pltpu.VMEM((dv, 2 * tq), F32),
        ],
        compiler_params=_params("arbitrary", "arbitrary"),
        name="diff_attn",
    )(proj, proj, proj, bias_tiles, lam_p, g)


def _mix_cross_kernel(a_ref, d_ref, x_ref, wo_ref, gc_ref, wcq_ref, ckv_ref, wco_ref, gf_ref,
                      xo_ref, hf_ref):
    x1 = x_ref[...] + _dot(a_ref[...], wo_ref[0:C_A, :]) + _dot(d_ref[...], wo_ref[C_A:, :])
    hc = _rms(x1, gc_ref[...], NORM_EPS).astype(BF16)
    cq = _dot(hc, wcq_ref[...]).astype(BF16)
    kv_off = H_CROSS * DH_CROSS
    heads = []
    for h in range(H_CROSS):
        lo, hi = h * DH_CROSS, (h + 1) * DH_CROSS
        cl = _dot_nt(cq[:, lo:hi], ckv_ref[:, lo:hi]) * (DH_CROSS ** -0.5)
        e = jnp.exp(cl - jnp.max(cl, axis=-1, keepdims=True))
        l = jnp.sum(e, axis=-1, keepdims=True)
        heads.append((_dot(e.astype(BF16), ckv_ref[:, kv_off + lo:kv_off + hi]) / l).astype(BF16))
    x2 = x1 + _dot(jnp.concatenate(heads, axis=1), wco_ref[...])
    xo_ref[...] = x2
    hf_ref[...] = _rms(x2, gf_ref[...], NORM_EPS).astype(hf_ref.dtype)


def _mix_cross(a, d, x, w_out, g_cross, w_cq, ckv, w_co, g_ffn, *, layer, tm):
    s, d_model = x.shape
    n_cq = w_cq.shape[2]
    const = lambda i: (0, 0)
    of_layer = lambda i: (layer, 0, 0)
    resident = pl.Buffered(1)
    return pl.pallas_call(
        _mix_cross_kernel,
        out_shape=(jax.ShapeDtypeStruct((s, d_model), F32), jax.ShapeDtypeStruct((s, d_model), BF16)),
        grid=(s // tm,),
        in_specs=[
            pl.BlockSpec((tm, C_A), lambda i: (i, 0)),
            pl.BlockSpec((tm, V_WIDTH), lambda i: (i, 0)),
            pl.BlockSpec((tm, d_model), lambda i: (i, 0)),
            pl.BlockSpec((None, C_A + V_WIDTH, d_model), of_layer, pipeline_mode=resident),
            pl.BlockSpec((1, d_model), const),
            pl.BlockSpec((None, d_model, n_cq), of_layer, pipeline_mode=resident),
            pl.BlockSpec((None,) + ckv.shape[1:], of_layer, pipeline_mode=resident),
            pl.BlockSpec((None, n_cq, d_model), of_layer, pipeline_mode=resident),
            pl.BlockSpec((1, d_model), const),
        ],
        out_specs=(pl.BlockSpec((tm, d_model), lambda i: (i, 0)),
                   pl.BlockSpec((tm, d_model), lambda i: (i, 0))),
        compiler_params=_params("arbitrary"),
        name="mix_cross",
    )(a, d, x, w_out, g_cross, w_cq, ckv, w_co, g_ffn)


def _ffn_kernel(hf_ref, halo_ref, x_ref, wg_ref, wv_ref, cwg_ref, cwv_ref, cbg_ref, cbv_ref, wd_ref,
                gout_ref, o_ref, ext_ref, ug_ref, uv_ref, *, tm, norm_output):
    @pl.when(pl.program_id(1) == 0)
    def _():
        halo = halo_ref[...]
        ext_ref[0:HALO_F, :] = jnp.where(pl.program_id(0) > 0, halo, jnp.zeros_like(halo))
        ext_ref[HALO_F:, :] = hf_ref[...]
        o_ref[...] = x_ref[...]

    ext = ext_ref[...]
    ug_ref[...] = _dot(ext, wg_ref[...])
    uv_ref[...] = _dot(ext, wv_ref[...])

    def conv(u_ref, cw_ref, cb_ref, row0, rows):
        out = cb_ref[...]
        for j in range(CONV_F_WIDTH):
            out = out + cw_ref[j:j + 1, :] * u_ref[pl.ds(HALO_F - (CONV_F_WIDTH - 1) + j + row0, rows), :]
        return out

    rows = tm // FFN_ROW_PIECES
    for r in range(FFN_ROW_PIECES):
        y = (jax.nn.silu(conv(ug_ref, cwg_ref, cbg_ref, r * rows, rows))
             * conv(uv_ref, cwv_ref, cbv_ref, r * rows, rows))
        o_ref[r * rows:(r + 1) * rows, :] += _dot(y.astype(BF16), wd_ref[...])

    if norm_output:
        @pl.when(pl.program_id(1) == pl.num_programs(1) - 1)
        def _():
            o_ref[...] = _rms(o_ref[...], gout_ref[...], NORM_EPS)


def _ffn(hf, x, w_up, cw, cb, w_down, g_out, *, layer, tm, tf, norm_output):
    s, d_model = x.shape
    d_ff = w_down.shape[1]
    nf = d_ff // tf
    halo_blocks = tm // HALO_F
    return pl.pallas_call(
        functools.partial(_ffn_kernel, tm=tm, norm_output=norm_output),
        out_shape=jax.ShapeDtypeStruct((s, d_model), F32),
        grid=(s // tm, nf),
        in_specs=[
            pl.BlockSpec((tm, d_model), lambda i, f: (i, 0)),
            pl.BlockSpec((HALO_F, d_model), lambda i, f: (jnp.maximum(i * halo_blocks - 1, 0), 0)),
            pl.BlockSpec((tm, d_model), lambda i, f: (i, 0)),
            pl.BlockSpec((None, d_model, tf), lambda i, f: (layer, 0, f)),
            pl.BlockSpec((None, d_model, tf), lambda i, f: (layer, 0, nf + f)),
            pl.BlockSpec((CONV_F_WIDTH, tf), lambda i, f: (0, f)),
            pl.BlockSpec((CONV_F_WIDTH, tf), lambda i, f: (0, nf + f)),
            pl.BlockSpec((1, tf), lambda i, f: (0, f)),
            pl.BlockSpec((1, tf), lambda i, f: (0, nf + f)),
            pl.BlockSpec((None, tf, d_model), lambda i, f: (layer, f, 0)),
            pl.BlockSpec((1, d_model), lambda i, f: (0, 0)),
        ],
        out_specs=pl.BlockSpec((tm, d_model), lambda i, f: (i, 0)),
        scratch_shapes=[
            pltpu.VMEM((tm + HALO_F, d_model), BF16),
            pltpu.VMEM((tm + HALO_F, tf), F32),
            pltpu.VMEM((tm + HALO_F, tf), F32),
        ],
        compiler_params=_params("arbitrary", "arbitrary"),
        name="conv_ffn",
    )(hf, hf, x, w_up, w_up, cw, cw, cb, cb, w_down, g_out)


def kernel(x, mem, rel_bias_table, g_mix, w_in, conv_a_w, conv_a_b, ln_a_g, ln_a_b, diff_lambda, subln_g, w_out, g_cross, g_mem, w_cq, w_ckv, w_co, g_ffn, w_up, conv_f_w, conv_f_b, w_down, g_final):
    b, s, d_model = x.shape
    assert b == 1 and mem.shape[0] == 1
    depth = w_in.shape[0]
    d_ff = w_down.shape[1]

    tk_attn = _pick_tile(s, 512)
    tq_attn = _pick_tile(s, 1024)
    assert tk_attn >= MAX_DISTANCE, "the two biased key blocks must cover every bucketed distance"
    tm_proj = _pick_tile(s, 1024)
    tn_proj = _pick_tile(w_in.shape[2], 1024)
    ts_conv = _pick_tile(s, 512)
    tm_mix = _pick_tile(s, 512)
    tm_ffn = _pick_tile(s, 512)
    tf_ffn = _pick_tile(d_ff, 512)

    row = lambda p: p.reshape(depth, 1, p.shape[-1])
    w_in_b, w_out_b, w_cq_b = w_in.astype(BF16), w_out.astype(BF16), w_cq.astype(BF16)
    w_ckv_b, w_co_b = w_ckv.astype(BF16), w_co.astype(BF16)
    w_up_b, w_down_b = w_up.astype(BF16), w_down.astype(BF16)
    g_mix, g_cross, g_ffn, g_mem = row(g_mix), row(g_cross), row(g_ffn), row(g_mem)
    conv_a_b, ln_a_g, ln_a_b = row(conv_a_b), row(ln_a_g), row(ln_a_b)
    subln_g, conv_f_b = row(subln_g), row(conv_f_b)

    bias_tiles = _bias_tiles(rel_bias_table, tk_attn)
    ckv = _mem_kv(mem[0], g_mem, w_ckv_b)

    xs = x[0]
    for l in range(depth):
        lambda_init = 0.8 - 0.6 * math.exp(-0.3 * l)
        proj = _in_proj(xs, g_mix[l], w_in_b, layer=l, tm=tm_proj, tn=tn_proj)
        a = _conv_module(proj, conv_a_w[l], conv_a_b[l], ln_a_g[l], ln_a_b[l], ts=ts_conv)
        d = _diff_attn(proj, bias_tiles, diff_lambda[l], subln_g[l], tq=tq_attn, tk=tk_attn,
                       lambda_init=lambda_init)
        xs, hf = _mix_cross(a, d, xs, w_out_b, g_cross[l], w_cq_b, ckv, w_co_b, g_ffn[l],
                            layer=l, tm=tm_mix)
        xs = _ffn(hf, xs, w_up_b, conv_f_w[l], conv_f_b[l], w_down_b, g_final.reshape(1, d_model),
                  layer=l, tm=tm_ffn, tf=tf_ffn, norm_output=(l == depth - 1))
    return xs[None]
```
